```python
import math
import jax, jax.numpy as jnp
from jax import lax
import numpy as np

D_MODEL = 1024
BATCH = 16
SEQ = 4096
DEPTH = 2

CHUNK = 64
Q_BLOCK = 128
POOL_WINDOWS = (2, 4, 8, 16)
POOL_GROUP = D_MODEL // 8
POOL_WIDTH = POOL_GROUP * len(POOL_WINDOWS)
MLSTM_HEADS = 4
MLSTM_HEAD_DIM = D_MODEL // 8
MLSTM_WIDTH = MLSTM_HEADS * MLSTM_HEAD_DIM
MLSTM_CONV = 4
EVEN_IN = POOL_WIDTH + 4 * MLSTM_WIDTH + 2 * MLSTM_HEADS
MIX_WIDTH = POOL_WIDTH + MLSTM_WIDTH
MLA_HEADS = 8
MLA_NOPE = 128
MLA_ROPE = 64
MLA_V = 128
MLA_Q_LORA = D_MODEL // 2
MLA_KV_LORA = D_MODEL // 4
ODD_IN = MLA_Q_LORA + MLA_KV_LORA + MLA_ROPE
ROPE_THETA = 10000.0
D_FF = 2816
FFN_CONV = 3
N_EVEN = (DEPTH + 1) // 2
N_ODD = DEPTH // 2
DEEPNORM_ALPHA = (2 * DEPTH) ** 0.25
DEEPNORM_BETA = (8 * DEPTH) ** -0.25
LN_EPS = 1e-5
RMS_EPS = 1e-6

kernel_name = 'hybrid_pool_mlstm_mla_convffn_trunk'


def layer_norm(x, g, b):
    xf = x.astype(jnp.float32)
    mu = jnp.mean(xf, -1, keepdims=True)
    var = jnp.mean(jnp.square(xf - mu), -1, keepdims=True)
    return ((xf - mu) * lax.rsqrt(var + LN_EPS) * g + b).astype(x.dtype)


def rms_norm(x, g):
    xf = x.astype(jnp.float32)
    return (xf * lax.rsqrt(jnp.mean(xf * xf, -1, keepdims=True) + RMS_EPS) * g).astype(x.dtype)


def causal_dwconv(x, w):
    k = w.shape[0]
    return lax.conv_general_dilated(
        x, w[:, None, :].astype(x.dtype), window_strides=(1,), padding=[(k - 1, 0)],
        dimension_numbers=('NWC', 'WIO', 'NWC'), feature_group_count=x.shape[-1])


def ada_modulation(c, w, b):
    mod = jnp.einsum('bd,de->be', jax.nn.silu(c), w) + b
    shift, scale, gate = jnp.split(mod, 3, axis=-1)
    return shift[:, None], scale[:, None], gate[:, None]


def post_norm_residual(x, y, gate, g, b):
    return layer_norm(DEEPNORM_ALPHA * x + gate * y, g, b)


def rope_tables(positions):
    half = MLA_ROPE // 2
    inv = ROPE_THETA ** (-jnp.arange(half, dtype=jnp.float32) / half)
    ang = positions.astype(jnp.float32)[..., None] * inv
    return jnp.cos(ang), jnp.sin(ang)


def apply_rope(x, cos, sin):
    x1, x2 = jnp.split(x.astype(jnp.float32), 2, axis=-1)
    return jnp.concatenate([x1 * cos - x2 * sin, x1 * sin + x2 * cos], -1).astype(x.dtype)


def multiscale_pool(u, w_group, scale):
    b, s, _ = u.shape
    uf = u.astype(jnp.float32).reshape(b, s, len(POOL_WINDOWS), POOL_GROUP)
    cs = jnp.cumsum(uf, axis=1)
    t = jnp.arange(1, s + 1, dtype=jnp.float32)
    outs = []
    for gi, win in enumerate(POOL_WINDOWS):
        csg = cs[:, :, gi]
        lag = jnp.pad(csg, ((0, 0), (win, 0), (0, 0)))[:, :s]
        mean = (csg - lag) / jnp.minimum(t, float(win))[None, :, None]
        outs.append(mean - uf[:, :, gi])
    pooled = jnp.stack(outs, axis=2).astype(u.dtype)
    mixed = jnp.einsum('bsgc,gcd->bsgd', pooled, w_group)
    return mixed.reshape(b, s, POOL_WIDTH) * scale


def mlstm_chunkwise(q, k, v, i_pre, f_pre):
    b, s, h, dh = q.shape
    nc = s // CHUNK

    def to_chunks(a):
        a = a.astype(jnp.float32).reshape((b, nc, CHUNK, h) + a.shape[3:])
        return jnp.moveaxis(jnp.moveaxis(a, 1, 0), 3, 2)

    log_f = jax.nn.log_sigmoid(f_pre.astype(jnp.float32))
    xs = (to_chunks(q), to_chunks(k), to_chunks(v), to_chunks(i_pre), to_chunks(log_f))
    causal = jnp.tril(jnp.ones((CHUNK, CHUNK), dtype=bool))

    def step(carry, inp):
        c_mat, n_vec, m = carry
        qc, kc, vc, ic, lfc = inp
        bcum = jnp.cumsum(lfc, axis=-1)
        dmat = jnp.where(causal, bcum[..., :, None] - bcum[..., None, :] + ic[..., None, :], -jnp.inf)
        m_inter = bcum + m[..., None]
        m_t = jnp.maximum(m_inter, jnp.max(dmat, -1))
        decay = jnp.exp(dmat - m_t[..., None])
        inter = jnp.exp(m_inter - m_t)
        scores = jnp.einsum('bhtd,bhsd->bhts', qc, kc) * decay
        num = (jnp.einsum('bhts,bhse->bhte', scores, vc)
               + inter[..., None] * jnp.einsum('bhtd,bhde->bhte', qc, c_mat))
        den = jnp.sum(scores, -1) + inter * jnp.einsum('bhtd,bhd->bht', qc, n_vec)
        hc = num / jnp.maximum(jnp.abs(den), jnp.exp(-m_t))[..., None]
        b_last = bcum[..., -1]
        g = b_last[..., None] - bcum + ic
        m_new = jnp.maximum(b_last + m, jnp.max(g, -1))
        wk = jnp.exp(g - m_new[..., None])
        carry_scale = jnp.exp(b_last + m - m_new)
        c_mat = carry_scale[..., None, None] * c_mat + jnp.einsum('bhs,bhsd,bhse->bhde', wk, kc, vc)
        n_vec = carry_scale[..., None] * n_vec + jnp.einsum('bhs,bhsd->bhd', wk, kc)
        return (c_mat, n_vec, m_new), hc

    init = (jnp.zeros((b, h, dh, dh), jnp.float32), jnp.zeros((b, h, dh), jnp.float32),
            jnp.zeros((b, h), jnp.float32))
    _, hs = lax.scan(step, init, xs)
    return jnp.transpose(hs, (1, 0, 3, 2, 4)).reshape(b, s, h, dh)


def pool_mlstm_mixer(h, w_in, pool_w, pool_scale, conv_qk, gate_b, head_norm, w_out):
    b, s, _ = h.shape
    u = jnp.einsum('bsd,de->bse', h, w_in)
    u_pool, u_qk, u_v, u_o, u_if = jnp.split(
        u, [POOL_WIDTH, POOL_WIDTH + 2 * MLSTM_WIDTH, POOL_WIDTH + 3 * MLSTM_WIDTH,
            POOL_WIDTH + 4 * MLSTM_WIDTH], axis=-1)
    y_pool = multiscale_pool(u_pool, pool_w, pool_scale)
    qk = jax.nn.silu(causal_dwconv(u_qk, conv_qk))
    q, k = jnp.split(qk, 2, axis=-1)
    shp = (b, s, MLSTM_HEADS, MLSTM_HEAD_DIM)
    i_pre, f_pre = jnp.split(u_if.astype(jnp.float32) + gate_b, 2, axis=-1)
    hm = mlstm_chunkwise(q.reshape(shp), k.reshape(shp) * (MLSTM_HEAD_DIM ** -0.5),
                         u_v.reshape(shp), i_pre, f_pre)
    mu = jnp.mean(hm, -1, keepdims=True)
    var = jnp.mean(jnp.square(hm - mu), -1, keepdims=True)
    hm = ((hm - mu) * lax.rsqrt(var + LN_EPS)).reshape(b, s, MLSTM_WIDTH) * head_norm
    y_mlstm = (jax.nn.sigmoid(u_o.astype(jnp.float32)) * hm).astype(h.dtype)
    y = jnp.concatenate([y_pool, y_mlstm], axis=-1)
    return jnp.einsum('bse,ed->bsd', y, w_out)


def mla_mixer(h, cos, sin, w_in, q_norm, kv_norm, w_uq, w_ukv, w_out):
    b, s, _ = h.shape
    u = jnp.einsum('bsd,de->bse', h, w_in)
    c_q, c_kv, k_r = jnp.split(u, [MLA_Q_LORA, MLA_Q_LORA + MLA_KV_LORA], axis=-1)
    q = jnp.einsum('bsr,re->bse', rms_norm(c_q, q_norm), w_uq).reshape(b, s, MLA_HEADS, MLA_NOPE + MLA_ROPE)
    q_nope = q[..., :MLA_NOPE]
    q_rope = apply_rope(q[..., MLA_NOPE:], cos[:, :, None], sin[:, :, None])
    kv = jnp.einsum('bsr,re->bse', rms_norm(c_kv, kv_norm), w_ukv).reshape(b, s, MLA_HEADS, MLA_NOPE + MLA_V)
    k_nope, v = kv[..., :MLA_NOPE], kv[..., MLA_NOPE:]
    k_rope = apply_rope(k_r, cos, sin)
    scale = (MLA_NOPE + MLA_ROPE) ** -0.5
    chunk_id = jnp.arange(s) // CHUNK
    outs = []
    for start in range(0, s, Q_BLOCK):
        end = start + Q_BLOCK
        sc = (jnp.einsum('bqhd,bkhd->bhqk', q_nope[:, start:end], k_nope[:, :end])
              + jnp.einsum('bqhd,bkd->bhqk', q_rope[:, start:end], k_rope[:, :end]))
        mask = chunk_id[start:end, None] >= chunk_id[None, :end]
        sc = jnp.where(mask, sc.astype(jnp.float32) * scale, -jnp.inf)
        p = jax.nn.softmax(sc, axis=-1).astype(v.dtype)
        outs.append(jnp.einsum('bhqk,bkhd->bqhd', p, v[:, :end]))
    o = jnp.concatenate(outs, axis=1).reshape(b, s, MLA_HEADS * MLA_V)
    return jnp.einsum('bse,ed->bsd', o, w_out)


def conv_ffn(h, w_up, conv_w, w_down):
    a, g = jnp.split(jnp.einsum('bsd,df->bsf', h, w_up), 2, axis=-1)
    g = jax.nn.gelu(causal_dwconv(g, conv_w), approximate=False)
    return jnp.einsum('bsf,fd->bsd', a * g, w_down)


def setup_inputs(seed: int = 0) -> dict:
    key = jax.random.key(seed)
    keys = iter(jax.random.split(key, 48))

    def nrm(shape, std):
        return jax.random.normal(next(keys), shape, jnp.float32) * std

    d, ne, no = D_MODEL, N_EVEN, N_ODD
    x = nrm((BATCH, SEQ, d), 1.0)
    c = nrm((BATCH, d), 1.0)
    positions = (jnp.arange(SEQ, dtype=jnp.int32)[None, :]
                 + jax.random.randint(next(keys), (BATCH, 1), 0, 1024, dtype=jnp.int32))
    out_std = DEEPNORM_BETA * MIX_WIDTH ** -0.5
    inp = {'x': x, 'c': c, 'positions': positions}
    inp['e_ada_w'] = nrm((ne, d, 3 * d), d ** -0.5)
    inp['e_ada_b'] = nrm((ne, 3 * d), 0.02)
    inp['e_w_in'] = nrm((ne, d, EVEN_IN), d ** -0.5)
    inp['e_pool_w'] = nrm((ne, len(POOL_WINDOWS), POOL_GROUP, POOL_GROUP), POOL_GROUP ** -0.5)
    inp['e_pool_scale'] = 1.0 + nrm((ne, POOL_WIDTH), 0.1)
    inp['e_conv_qk'] = nrm((ne, MLSTM_CONV, 2 * MLSTM_WIDTH), MLSTM_CONV ** -0.5)
    gate_i = nrm((ne, MLSTM_HEADS), 0.1)
    gate_f = jnp.linspace(3.0, 6.0, MLSTM_HEADS, dtype=jnp.float32)[None] + nrm((ne, MLSTM_HEADS), 0.1)
    inp['e_gate_b'] = jnp.concatenate([gate_i, gate_f], axis=-1)
    inp['e_head_norm'] = 1.0 + nrm((ne, MLSTM_WIDTH), 0.02)
    inp['e_w_out'] = nrm((ne, MIX_WIDTH, d), out_std)
    inp['e_ln_g'] = 1.0 + nrm((ne, d), 0.02)
    inp['e_ln_b'] = nrm((ne, d), 0.02)
    inp['o_ada_w'] = nrm((no, d, 3 * d), d ** -0.5)
    inp['o_ada_b'] = nrm((no, 3 * d), 0.02)
    inp['o_w_in'] = nrm((no, d, ODD_IN), d ** -0.5)
    inp['o_q_norm'] = 1.0 + nrm((no, MLA_Q_LORA), 0.02)
    inp['o_kv_norm'] = 1.0 + nrm((no, MLA_KV_LORA), 0.02)
    inp['o_w_uq'] = nrm((no, MLA_Q_LORA, MLA_HEADS * (MLA_NOPE + MLA_ROPE)), MLA_Q_LORA ** -0.5)
    inp['o_w_ukv'] = nrm((no, MLA_KV_LORA, MLA_HEADS * (MLA_NOPE + MLA_V)), MLA_KV_LORA ** -0.5)
    inp['o_w_out'] = nrm((no, MLA_HEADS * MLA_V, d), DEEPNORM_BETA * (MLA_HEADS * MLA_V) ** -0.5)
    inp['o_ln_g'] = 1.0 + nrm((no, d), 0.02)
    inp['o_ln_b'] = nrm((no, d), 0.02)
    inp['f_ada_w'] = nrm((DEPTH, d, 3 * d), d ** -0.5)
    inp['f_ada_b'] = nrm((DEPTH, 3 * d), 0.02)
    inp['f_w_up'] = nrm((DEPTH, d, 2 * D_FF), d ** -0.5)
    inp['f_conv'] = nrm((DEPTH, FFN_CONV, D_FF), FFN_CONV ** -0.5)
    inp['f_w_down'] = nrm((DEPTH, D_FF, d), DEEPNORM_BETA * D_FF ** -0.5)
    inp['f_ln_g'] = 1.0 + nrm((DEPTH, d), 0.02)
    inp['f_ln_b'] = nrm((DEPTH, d), 0.02)
    return inp


def reference(x, c, positions,
              e_ada_w, e_ada_b, e_w_in, e_pool_w, e_pool_scale, e_conv_qk, e_gate_b,
              e_head_norm, e_w_out, e_ln_g, e_ln_b,
              o_ada_w, o_ada_b, o_w_in, o_q_norm, o_kv_norm, o_w_uq, o_w_ukv, o_w_out,
              o_ln_g, o_ln_b,
              f_ada_w, f_ada_b, f_w_up, f_conv, f_w_down, f_ln_g, f_ln_b):
    cos, sin = rope_tables(positions)
    for layer in range(DEPTH):
        j = layer // 2
        if layer % 2 == 0:
            shift, scale, gate = ada_modulation(c, e_ada_w[j], e_ada_b[j])
            y = pool_mlstm_mixer(x * (1.0 + scale) + shift, e_w_in[j], e_pool_w[j], e_pool_scale[j],
                                 e_conv_qk[j], e_gate_b[j], e_head_norm[j], e_w_out[j])
            x = post_norm_residual(x, y, gate, e_ln_g[j], e_ln_b[j])
        else:
            shift, scale, gate = ada_modulation(c, o_ada_w[j], o_ada_b[j])
            y = mla_mixer(x * (1.0 + scale) + shift, cos, sin, o_w_in[j], o_q_norm[j], o_kv_norm[j],
                          o_w_uq[j], o_w_ukv[j], o_w_out[j])
            x = post_norm_residual(x, y, gate, o_ln_g[j], o_ln_b[j])
        shift, scale, gate = ada_modulation(c, f_ada_w[layer], f_ada_b[layer])
        y = conv_ffn(x * (1.0 + scale) + shift, f_w_up[layer], f_conv[layer], f_w_down[layer])
        x = post_norm_residual(x, y, gate, f_ln_g[layer], f_ln_b[layer])
    return x
```

```python
import functools

import jax
import jax.numpy as jnp
from jax import lax
from jax.experimental import pallas as pl
from jax.experimental.pallas import tpu as pltpu

F32 = jnp.float32
BF16 = jnp.bfloat16
HIGHEST = lax.Precision.HIGHEST

D_MODEL = 1024
DEPTH = 2
POOL_WINDOWS = (2, 4, 8, 16)
POOL_GROUP = 128
POOL_WIDTH = 512
POOL_HALO = 16
MLSTM_HEADS = 4
MLSTM_HEAD_DIM = 128
MLSTM_WIDTH = 512
MLSTM_CONV = 4
CONV_HALO = 8
GATE_PAD = 128
MLA_HEADS = 8
MLA_NOPE = 128
MLA_ROPE = 64
MLA_V = 128
MLA_QK = MLA_NOPE + MLA_ROPE
MLA_Q_LORA = 512
MLA_KV_LORA = 256
ROPE_THETA = 10000.0
D_FF = 2816
FFN_CONV = 3
DEEPNORM_ALPHA = (2 * DEPTH) ** 0.25
LN_EPS = 1e-5
RMS_EPS = 1e-6

VMEM_CAP_BYTES = 64 * 1024 * 1024
LANES = 128
SUBLANES = 8

SEQ_TILE = 512
MLSTM_CHUNK = 256
ATT_TILE = 512
FFN_TILE = 256
ADA_TILE = 512


def _params(semantics, vmem_mib):
    limit = vmem_mib * 1024 * 1024
    assert limit <= VMEM_CAP_BYTES
    return pltpu.CompilerParams(dimension_semantics=semantics, vmem_limit_bytes=limit)


def _const_spec(shape):
    nd = len(shape)
    return pl.BlockSpec(shape, lambda *_: (0,) * nd, pipeline_mode=pl.Buffered(1))


def _layer_norm_rows(z, g, b):
    mu = jnp.mean(z, axis=-1, keepdims=True)
    zc = z - mu
    var = jnp.mean(zc * zc, axis=-1, keepdims=True)
    return zc * lax.rsqrt(var + LN_EPS) * g + b


def _sigmoid(x):
    return 1.0 / (1.0 + jnp.exp(-x))


def _ada_kernel(c_ref, w_ref, b_ref, o_ref):
    c = c_ref[...]
    cs = c * _sigmoid(c)
    o_ref[...] = jnp.dot(cs, w_ref[...], precision=HIGHEST, preferred_element_type=F32) + b_ref[...]


def _ada_modulation(c, w, b, layer):
    bsz, d = c.shape
    n_out = w.shape[-1]
    b3 = b.reshape(b.shape[0], 1, n_out)
    out = pl.pallas_call(
        _ada_kernel,
        grid=(n_out // ADA_TILE,),
        in_specs=[
            pl.BlockSpec((bsz, d), lambda n: (0, 0)),
            pl.BlockSpec((None, d, ADA_TILE), lambda n: (layer, 0, n)),
            pl.BlockSpec((None, 1, ADA_TILE), lambda n: (layer, 0, n)),
        ],
        out_specs=pl.BlockSpec((bsz, ADA_TILE), lambda n: (0, n)),
        out_shape=jax.ShapeDtypeStruct((bsz, n_out), F32),
        compiler_params=_params(("arbitrary",), 16),
        name="ada_modulation",
    )(c, w, b3)
    return out.reshape(bsz, 3, d)


def _even_proj_kernel(x_ref, mod_ref, w_ref, gb_ref, pw_ref, ps_ref, cw_ref,
                      ypool_ref, q_ref, k_ref, v_ref, o_ref, gcol_ref, grow_ref,
                      pbuf, qkbuf, *, ts):
    s = pl.program_id(1)

    @pl.when(s == 0)
    def _():
        pbuf[0:POOL_HALO, :] = jnp.zeros((POOL_HALO, POOL_WIDTH), F32)
        qkbuf[0:CONV_HALO, :] = jnp.zeros((CONV_HALO, 2 * MLSTM_WIDTH), F32)

    mod = mod_ref[0]
    shift, scale = mod[0:1], mod[1:2]
    h = (x_ref[0] * (1.0 + scale) + shift).astype(BF16)

    c0 = 0
    pbuf[POOL_HALO:POOL_HALO + ts, :] = jnp.dot(
        h, w_ref[:, c0:c0 + POOL_WIDTH], preferred_element_type=F32)
    c0 += POOL_WIDTH
    qkbuf[CONV_HALO:CONV_HALO + ts, :] = jnp.dot(
        h, w_ref[:, c0:c0 + 2 * MLSTM_WIDTH], preferred_element_type=F32)
    c0 += 2 * MLSTM_WIDTH
    v_ref[0] = jnp.dot(h, w_ref[:, c0:c0 + MLSTM_WIDTH], preferred_element_type=F32).astype(BF16)
    c0 += MLSTM_WIDTH
    o_ref[0] = _sigmoid(jnp.dot(h, w_ref[:, c0:c0 + MLSTM_WIDTH], preferred_element_type=F32))
    c0 += MLSTM_WIDTH
    gates = jnp.dot(h, w_ref[:, c0:c0 + GATE_PAD], preferred_element_type=F32) + gb_ref[...]

    lane = lax.broadcasted_iota(jnp.int32, gates.shape, 1)
    log_f = jnp.minimum(gates, 0.0) - jnp.log1p(jnp.exp(-jnp.abs(gates)))
    gates = jnp.where(lane >= MLSTM_HEADS, log_f, gates)
    gcol_ref[0] = gates
    grow_ref[0] = gates.T[0:2 * MLSTM_HEADS, :]

    t_glob = s * ts + lax.broadcasted_iota(jnp.int32, (ts, 1), 0)
    for gi, win in enumerate(POOL_WINDOWS):
        cols = slice(gi * POOL_GROUP, (gi + 1) * POOL_GROUP)
        cur = pbuf[POOL_HALO:POOL_HALO + ts, cols]
        acc = cur
        for j in range(1, win):
            acc = acc + pbuf[POOL_HALO - j:POOL_HALO - j + ts, cols]
        cnt = jnp.minimum(t_glob + 1, win).astype(F32)
        pooled = acc / cnt - cur
        mixed = jnp.dot(pooled.astype(BF16), pw_ref[gi], preferred_element_type=F32)
        ypool_ref[0, :, cols] = (mixed * ps_ref[:, cols]).astype(BF16)
    pbuf[0:POOL_HALO, :] = pbuf[ts:ts + POOL_HALO, :]

    cw = cw_ref[...]
    acc = qkbuf[CONV_HALO:CONV_HALO + ts, :] * cw[MLSTM_CONV - 1:MLSTM_CONV]
    for j in range(MLSTM_CONV - 1):
        off = CONV_HALO - (MLSTM_CONV - 1) + j
        acc = acc + qkbuf[off:off + ts, :] * cw[j:j + 1]
    qk = acc * _sigmoid(acc)
    q_ref[0] = qk[:, :MLSTM_WIDTH].astype(BF16)
    k_ref[0] = (qk[:, MLSTM_WIDTH:] * (MLSTM_HEAD_DIM ** -0.5)).astype(BF16)
    qkbuf[0:CONV_HALO, :] = qkbuf[ts:ts + CONV_HALO, :]


def _even_proj(x, mod, w_cat, gate_b, pool_w, pool_scale, conv_qk):
    bsz, seq, d = x.shape
    ts = SEQ_TILE
    n_cat = w_cat.shape[1]
    tok = lambda width: pl.BlockSpec((1, ts, width), lambda b, s: (b, s, 0))
    out_shape = (
        jax.ShapeDtypeStruct((bsz, seq, POOL_WIDTH), BF16),
        jax.ShapeDtypeStruct((bsz, seq, MLSTM_WIDTH), BF16),
        jax.ShapeDtypeStruct((bsz, seq, MLSTM_WIDTH), BF16),
        jax.ShapeDtypeStruct((bsz, seq, MLSTM_WIDTH), BF16),
        jax.ShapeDtypeStruct((bsz, seq, MLSTM_WIDTH), F32),
        jax.ShapeDtypeStruct((bsz, seq, GATE_PAD), F32),
        jax.ShapeDtypeStruct((bsz, 2 * MLSTM_HEADS, seq), F32),
    )
    return pl.pallas_call(
        functools.partial(_even_proj_kernel, ts=ts),
        grid=(bsz, seq // ts),
        in_specs=[
            tok(d),
            pl.BlockSpec((1, 3, d), lambda b, s: (b, 0, 0)),
            _const_spec((d, n_cat)),
            _const_spec((1, GATE_PAD)),
            _const_spec(pool_w.shape),
            _const_spec((1, POOL_WIDTH)),
            _const_spec(conv_qk.shape),
        ],
        out_specs=(
            tok(POOL_WIDTH), tok(MLSTM_WIDTH), tok(MLSTM_WIDTH), tok(MLSTM_WIDTH), tok(MLSTM_WIDTH),
            tok(GATE_PAD),
            pl.BlockSpec((1, 2 * MLSTM_HEADS, ts), lambda b, s: (b, 0, s)),
        ),
        out_shape=out_shape,
        scratch_shapes=[
            pltpu.VMEM((POOL_HALO + ts, POOL_WIDTH), F32),
            pltpu.VMEM((CONV_HALO + ts, 2 * MLSTM_WIDTH), F32),
        ],
        compiler_params=_params(("parallel", "arbitrary"), 48),
        name="even_proj",
    )(x, mod, w_cat, gate_b, pool_w, pool_scale, conv_qk)


def _mlstm_kernel(q_ref, k_ref, v_ref, o_ref, gc_ref, gr_ref, hn_ref, y_ref,
                  c_sc, n_sc, m_sc, *, chunk):
    s = pl.program_id(1)

    @pl.when(s == 0)
    def _():
        c_sc[...] = jnp.zeros(c_sc.shape, F32)
        n_sc[...] = jnp.zeros(n_sc.shape, F32)
        m_sc[...] = jnp.zeros(m_sc.shape, F32)

    gc = gc_ref[0]
    gr = gr_ref[0]
    row = lax.broadcasted_iota(jnp.int32, (chunk, chunk), 0)
    col = lax.broadcasted_iota(jnp.int32, (chunk, chunk), 1)
    causal = row >= col
    bc_col = jnp.dot(causal.astype(F32), gc, precision=HIGHEST, preferred_element_type=F32)
    bc_row = jnp.dot(gr, (row <= col).astype(F32), precision=HIGHEST, preferred_element_type=F32)

    nt = (((1,), (1,)), ((), ()))
    tn = (((0,), (0,)), ((), ()))
    for hd in range(MLSTM_HEADS):
        cols = slice(hd * MLSTM_HEAD_DIM, (hd + 1) * MLSTM_HEAD_DIM)
        i_col = gc[:, hd:hd + 1]
        b_col = bc_col[:, MLSTM_HEADS + hd:MLSTM_HEADS + hd + 1]
        i_row = gr[hd:hd + 1, :]
        b_row = bc_row[MLSTM_HEADS + hd:MLSTM_HEADS + hd + 1, :]
        m_prev = m_sc[hd][0:1, 0:1]
        c_prev = c_sc[hd]
        n_prev = n_sc[hd]
        qh = q_ref[0, :, cols]
        kh = k_ref[0, :, cols]
        vh = v_ref[0, :, cols]

        dmat = jnp.where(causal, b_col - b_row + i_row, -jnp.inf)
        m_inter = b_col + m_prev
        m_t = jnp.maximum(m_inter, jnp.max(dmat, axis=-1, keepdims=True))
        decay = jnp.exp(dmat - m_t)
        inter = jnp.exp(m_inter - m_t)
        scores = lax.dot_general(qh, kh, nt, preferred_element_type=F32) * decay
        num = (jnp.dot(scores.astype(BF16), vh, preferred_element_type=F32)
               + inter * jnp.dot(qh, c_prev.astype(BF16), preferred_element_type=F32))
        den = (jnp.sum(scores, axis=-1, keepdims=True)
               + inter * jnp.sum(qh.astype(F32) * n_prev, axis=-1, keepdims=True))
        hc = num / jnp.maximum(jnp.abs(den), jnp.exp(-m_t))

        mu = jnp.mean(hc, axis=-1, keepdims=True)
        hz = hc - mu
        var = jnp.mean(hz * hz, axis=-1, keepdims=True)
        hn = hz * lax.rsqrt(var + LN_EPS) * hn_ref[:, cols]
        y_ref[0, :, cols] = (o_ref[0, :, cols] * hn).astype(BF16)

        b_last = b_col[chunk - 1:chunk, :]
        g = b_last - b_col + i_col
        m_new = jnp.maximum(b_last + m_prev, jnp.max(g, axis=0, keepdims=True))
        wk = jnp.exp(g - m_new)
        carry = jnp.exp(b_last + m_prev - m_new)
        wv = (wk * vh.astype(F32)).astype(BF16)
        c_sc[hd] = carry * c_prev + lax.dot_general(kh, wv, tn, preferred_element_type=F32)
        n_sc[hd] = carry * n_prev + jnp.sum(wk * kh.astype(F32), axis=0, keepdims=True)
        m_sc[hd] = jnp.broadcast_to(m_new, (SUBLANES, LANES))


def _mlstm(q, k, v, o_sig, g_col, g_row, head_norm):
    bsz, seq, width = q.shape
    chunk = MLSTM_CHUNK
    tok = lambda w: pl.BlockSpec((1, chunk, w), lambda b, s: (b, s, 0))
    return pl.pallas_call(
        functools.partial(_mlstm_kernel, chunk=chunk),
        grid=(bsz, seq // chunk),
        in_specs=[
            tok(width), tok(width), tok(width), tok(width), tok(GATE_PAD),
            pl.BlockSpec((1, 2 * MLSTM_HEADS, chunk), lambda b, s: (b, 0, s)),
            _const_spec((1, width)),
        ],
        out_specs=tok(width),
        out_shape=jax.ShapeDtypeStruct((bsz, seq, width), BF16),
        scratch_shapes=[
            pltpu.VMEM((MLSTM_HEADS, MLSTM_HEAD_DIM, MLSTM_HEAD_DIM), F32),
            pltpu.VMEM((MLSTM_HEADS, 1, MLSTM_HEAD_DIM), F32),
            pltpu.VMEM((MLSTM_HEADS, SUBLANES, LANES), F32),
        ],
        compiler_params=_params(("parallel", "arbitrary"), 32),
        name="mlstm",
    )(q, k, v, o_sig, g_col, g_row, head_norm)


def _outproj_norm_kernel(*refs, widths):
    x_ref, mod_ref, g_ref, b_ref, w_ref = refs[:5]
    y_refs = refs[5:5 + len(widths)]
    out_ref = refs[5 + len(widths)]
    y = None
    r0 = 0
    for y_ref, width in zip(y_refs, widths):
        part = jnp.dot(y_ref[0], w_ref[r0:r0 + width, :], preferred_element_type=F32)
        y = part if y is None else y + part
        r0 += width
    gate = mod_ref[0][2:3]
    z = DEEPNORM_ALPHA * x_ref[0] + gate * y
    out_ref[0] = _layer_norm_rows(z, g_ref[...], b_ref[...])


def _outproj_norm(x, mod, ln_g, ln_b, w_out, ys):
    bsz, seq, d = x.shape
    ts = SEQ_TILE
    widths = tuple(y.shape[-1] for y in ys)
    tok = lambda w: pl.BlockSpec((1, ts, w), lambda b, s: (b, s, 0))
    return pl.pallas_call(
        functools.partial(_outproj_norm_kernel, widths=widths),
        grid=(bsz, seq // ts),
        in_specs=[
            tok(d),
            pl.BlockSpec((1, 3, d), lambda b, s: (b, 0, 0)),
            _const_spec((1, d)), _const_spec((1, d)),
            _const_spec(w_out.shape),
        ] + [tok(w) for w in widths],
        out_specs=tok(d),
        out_shape=jax.ShapeDtypeStruct((bsz, seq, d), F32),
        compiler_params=_params(("parallel", "arbitrary"), 32),
        name="outproj_norm",
    )(x, mod, ln_g, ln_b, w_out, *ys)


def _mla_proj_kernel(x_ref, mod_ref, pos_ref, inv_ref, w_in_ref, qn_ref, kvn_ref, wq_ref, wkv_ref,
                     q_out, k_out, v_out, *, ts):
    mod = mod_ref[0]
    shift, scale = mod[0:1], mod[1:2]
    h = (x_ref[0] * (1.0 + scale) + shift).astype(BF16)
    u = jnp.dot(h, w_in_ref[...], preferred_element_type=F32)

    def rms(a, g):
        return a * lax.rsqrt(jnp.mean(a * a, axis=-1, keepdims=True) + RMS_EPS) * g

    cq = rms(u[:, :MLA_Q_LORA], qn_ref[...]).astype(BF16)
    ckv = rms(u[:, MLA_Q_LORA:MLA_Q_LORA + MLA_KV_LORA], kvn_ref[...]).astype(BF16)
    kr = u[:, MLA_Q_LORA + MLA_KV_LORA:]

    pos = jnp.broadcast_to(pos_ref[0], (LANES, ts)).T
    ang = pos * inv_ref[...]
    lane = lax.broadcasted_iota(jnp.int32, (1, LANES), 1)
    sign = jnp.where(lane % MLA_ROPE < MLA_ROPE // 2, -1.0, 1.0)
    cos = jnp.cos(ang)
    sin = jnp.sin(ang) * sign

    sm_scale = MLA_QK ** -0.5
    n_nope = MLA_HEADS * MLA_NOPE
    n_rope = MLA_HEADS * MLA_ROPE
    qa = jnp.dot(cq, wq_ref[...], preferred_element_type=F32) * sm_scale
    reps = n_rope // LANES
    q_rope = (qa[:, n_nope:n_nope + n_rope] * jnp.concatenate([cos] * reps, axis=1)
              + qa[:, n_nope + n_rope:] * jnp.concatenate([sin] * reps, axis=1))
    k_rope = (kr[:, :MLA_ROPE] * cos[:, :MLA_ROPE] + kr[:, MLA_ROPE:] * sin[:, :MLA_ROPE]).astype(BF16)
    kv = jnp.dot(ckv, wkv_ref[...], preferred_element_type=F32)

    for hd in range(MLA_HEADS):
        q_out[0, hd, :, 0:MLA_NOPE] = qa[:, hd * MLA_NOPE:(hd + 1) * MLA_NOPE].astype(BF16)
        q_out[0, hd, :, MLA_NOPE:MLA_QK] = q_rope[:, hd * MLA_ROPE:(hd + 1) * MLA_ROPE].astype(BF16)
        k_out[0, hd, :, 0:MLA_NOPE] = kv[:, hd * MLA_NOPE:(hd + 1) * MLA_NOPE].astype(BF16)
        k_out[0, hd, :, MLA_NOPE:MLA_QK] = k_rope
        v_out[0, hd] = kv[:, n_nope + hd * MLA_V:n_nope + (hd + 1) * MLA_V].astype(BF16)


def _mla_proj(x, mod, pos, inv, w_in, q_norm, kv_norm, w_uq, w_ukv):
    bsz, seq, d = x.shape
    ts = SEQ_TILE
    head_spec = lambda w: pl.BlockSpec((1, MLA_HEADS, ts, w), lambda b, s: (b, 0, s, 0))
    return pl.pallas_call(
        functools.partial(_mla_proj_kernel, ts=ts),
        grid=(bsz, seq // ts),
        in_specs=[
            pl.BlockSpec((1, ts, d), lambda b, s: (b, s, 0)),
            pl.BlockSpec((1, 3, d), lambda b, s: (b, 0, 0)),
            pl.BlockSpec((1, 1, ts), lambda b, s: (b, 0, s)),
            _const_spec((1, LANES)),
            _const_spec(w_in.shape),
            _const_spec((1, MLA_Q_LORA)), _const_spec((1, MLA_KV_LORA)),
            _const_spec(w_uq.shape), _const_spec(w_ukv.shape),
        ],
        out_specs=(head_spec(MLA_QK), head_spec(MLA_QK), head_spec(MLA_V)),
        out_shape=(
            jax.ShapeDtypeStruct((bsz, MLA_HEADS, seq, MLA_QK), BF16),
            jax.ShapeDtypeStruct((bsz, MLA_HEADS, seq, MLA_QK), BF16),
            jax.ShapeDtypeStruct((bsz, MLA_HEADS, seq, MLA_V), BF16),
        ),
        compiler_params=_params(("parallel", "arbitrary"), 48),
        name="mla_proj",
    )(x, mod, pos, inv, w_in, q_norm, kv_norm, w_uq, w_ukv)


def _attn_kernel(q_ref, k_ref, v_ref, o_ref, m_sc, l_sc, acc_sc, *, tile, chunk):
    qi = pl.program_id(2)
    q = q_ref[0, 0]
    m_sc[...] = jnp.full(m_sc.shape, -jnp.inf, F32)
    l_sc[...] = jnp.zeros(l_sc.shape, F32)
    acc_sc[...] = jnp.zeros(acc_sc.shape, F32)
    nt = (((1,), (1,)), ((), ()))

    def update(k0, diagonal):
        k = k_ref[0, 0, pl.ds(k0, tile), :]
        v = v_ref[0, 0, pl.ds(k0, tile), :]
        sc = lax.dot_general(q, k, nt, preferred_element_type=F32)
        if diagonal:
            rq = lax.broadcasted_iota(jnp.int32, sc.shape, 0) // chunk
            rk = lax.broadcasted_iota(jnp.int32, sc.shape, 1) // chunk
            sc = jnp.where(rq >= rk, sc, -jnp.inf)
        m_prev = m_sc[...]
        m_new = jnp.maximum(m_prev, jnp.max(sc, axis=-1, keepdims=True))
        alpha = jnp.exp(m_prev - m_new)
        p = jnp.exp(sc - m_new)
        l_sc[...] = alpha * l_sc[...] + jnp.sum(p, axis=-1, keepdims=True)
        acc_sc[...] = alpha * acc_sc[...] + jnp.dot(p.astype(BF16), v, preferred_element_type=F32)
        m_sc[...] = m_new

    def body(j, carry):
        update(pl.multiple_of(j * tile, tile), False)
        return carry

    lax.fori_loop(0, qi, body, 0)
    update(pl.multiple_of(qi * tile, tile), True)
    o_ref[0] = (acc_sc[...] / l_sc[...]).astype(BF16)


def _attention(q, k, v, chunk):
    bsz, heads, seq, dqk = q.shape
    dv = v.shape[-1]
    tile = ATT_TILE
    return pl.pallas_call(
        functools.partial(_attn_kernel, tile=tile, chunk=chunk),
        grid=(bsz, heads, seq // tile),
        in_specs=[
            pl.BlockSpec((1, 1, tile, dqk), lambda b, h, i: (b, h, i, 0)),
            pl.BlockSpec((1, 1, seq, dqk), lambda b, h, i: (b, h, 0, 0)),
            pl.BlockSpec((1, 1, seq, dv), lambda b, h, i: (b, h, 0, 0)),
        ],
        out_specs=pl.BlockSpec((1, tile, dv), lambda b, h, i: (b, i, h)),
        out_shape=jax.ShapeDtypeStruct((bsz, seq, heads * dv), BF16),
        scratch_shapes=[
            pltpu.VMEM((tile, 1), F32),
            pltpu.VMEM((tile, 1), F32),
            pltpu.VMEM((tile, dv), F32),
        ],
        compiler_params=_params(("parallel", "parallel", "arbitrary"), 32),
        name="mla_attention",
    )(q, k, v)


def _ffn_kernel(x_ref, mod_ref, g_ref, b_ref, wup_ref, cw_ref, wdn_ref, out_ref, halo, *, ts, tf):
    s = pl.program_id(1)

    @pl.when(s == 0)
    def _():
        halo[...] = jnp.zeros(halo.shape, F32)

    mod = mod_ref[0]
    shift, scale, gate = mod[0:1], mod[1:2], mod[2:3]
    x = x_ref[0]
    h = (x * (1.0 + scale) + shift).astype(BF16)
    rowid = lax.broadcasted_iota(jnp.int32, (ts, tf), 0)
    sqrt_half = 0.5 ** 0.5
    y = None
    for f in range(D_FF // tf):
        cols = slice(f * tf, (f + 1) * tf)
        gcols = slice(D_FF + f * tf, D_FF + (f + 1) * tf)
        a = jnp.dot(h, wup_ref[:, cols], preferred_element_type=F32)
        g = jnp.dot(h, wup_ref[:, gcols], preferred_element_type=F32)
        prev = halo[:, cols]
        halo[:, cols] = g[ts - CONV_HALO:ts, :]
        p1 = prev[CONV_HALO - 1:CONV_HALO]
        p2 = prev[CONV_HALO - 2:CONV_HALO - 1]
        g1 = jnp.where(rowid == 0, p1, pltpu.roll(g, 1, axis=0))
        g2 = jnp.where(rowid == 0, p2, jnp.where(rowid == 1, p1, pltpu.roll(g, 2, axis=0)))
        cw = cw_ref[:, cols]
        gc = cw[0:1] * g2 + cw[1:2] * g1 + cw[2:3] * g
        act = a * (0.5 * gc * (1.0 + lax.erf(gc * sqrt_half)))
        part = jnp.dot(act.astype(BF16), wdn_ref[cols, :], preferred_element_type=F32)
        y = part if y is None else y + part
    z = DEEPNORM_ALPHA * x + gate * y
    out_ref[0] = _layer_norm_rows(z, g_ref[...], b_ref[...])


def _conv_ffn(x, mod, ln_g, ln_b, w_up, conv_w, w_down):
    bsz, seq, d = x.shape
    ts, tf = SEQ_TILE, FFN_TILE
    assert D_FF % tf == 0
    tok = pl.BlockSpec((1, ts, d), lambda b, s: (b, s, 0))
    return pl.pallas_call(
        functools.partial(_ffn_kernel, ts=ts, tf=tf),
        grid=(bsz, seq // ts),
        in_specs=[
            tok,
            pl.BlockSpec((1, 3, d), lambda b, s: (b, 0, 0)),
            _const_spec((1, d)), _const_spec((1, d)),
            _const_spec(w_up.shape), _const_spec(conv_w.shape), _const_spec(w_down.shape),
        ],
        out_specs=tok,
        out_shape=jax.ShapeDtypeStruct((bsz, seq, d), F32),
        scratch_shapes=[pltpu.VMEM((CONV_HALO, D_FF), F32)],
        compiler_params=_params(("parallel", "arbitrary"), 56),
        name="conv_ffn",
    )(x, mod, ln_g, ln_b, w_up, conv_w, w_down)


def _swap_halves(w):
    half = w.shape[-1] // 2
    return jnp.concatenate([w[..., half:], w[..., :half]], axis=-1)


def kernel(x, c, positions, e_ada_w, e_ada_b, e_w_in, e_pool_w, e_pool_scale, e_conv_qk, e_gate_b, e_head_norm, e_w_out, e_ln_g, e_ln_b, o_ada_w, o_ada_b, o_w_in, o_q_norm, o_kv_norm, o_w_uq, o_w_ukv, o_w_out, o_ln_g, o_ln_b, f_ada_w, f_ada_b, f_w_up, f_conv, f_w_down, f_ln_g, f_ln_b):
    bsz, seq, d = x.shape
    row = lambda a: a.reshape(1, -1)

    half = MLA_ROPE // 2
    inv = ROPE_THETA ** (-jnp.arange(half, dtype=F32) / half)
    inv_lanes = jnp.tile(inv, LANES // half).reshape(1, LANES)
    pos = positions.astype(F32).reshape(bsz, 1, seq)

    for layer in range(DEPTH):
        j = layer // 2
        if layer % 2 == 0:
            mod = _ada_modulation(c, e_ada_w, e_ada_b, j)
            n_main = POOL_WIDTH + 4 * MLSTM_WIDTH
            w_in = e_w_in[j]
            w_cat = jnp.pad(w_in, ((0, 0), (0, GATE_PAD - (w_in.shape[1] - n_main)))).astype(BF16)
            gate_b = jnp.pad(e_gate_b[j], (0, GATE_PAD - 2 * MLSTM_HEADS)).reshape(1, GATE_PAD)
            y_pool, q, k, v, o_sig, g_col, g_row = _even_proj(
                x, mod, w_cat, gate_b, e_pool_w[j].astype(BF16), row(e_pool_scale[j]), e_conv_qk[j])
            y_mlstm = _mlstm(q, k, v, o_sig, g_col, g_row, row(e_head_norm[j]))
            x = _outproj_norm(x, mod, row(e_ln_g[j]), row(e_ln_b[j]), e_w_out[j].astype(BF16),
                              (y_pool, y_mlstm))
        else:
            mod = _ada_modulation(c, o_ada_w, o_ada_b, j)
            w_in = o_w_in[j]
            k_r = w_in[:, MLA_Q_LORA + MLA_KV_LORA:]
            w_in_cat = jnp.concatenate([w_in, _swap_halves(k_r)], axis=1).astype(BF16)
            wq = o_w_uq[j].reshape(MLA_Q_LORA, MLA_HEADS, MLA_QK)
            wq_rope = wq[:, :, MLA_NOPE:]
            wq_cat = jnp.concatenate([
                wq[:, :, :MLA_NOPE].reshape(MLA_Q_LORA, -1),
                wq_rope.reshape(MLA_Q_LORA, -1),
                _swap_halves(wq_rope).reshape(MLA_Q_LORA, -1)], axis=1).astype(BF16)
            wkv = o_w_ukv[j].reshape(MLA_KV_LORA, MLA_HEADS, MLA_NOPE + MLA_V)
            wkv_cat = jnp.concatenate([
                wkv[:, :, :MLA_NOPE].reshape(MLA_KV_LORA, -1),
                wkv[:, :, MLA_NOPE:].reshape(MLA_KV_LORA, -1)], axis=1).astype(BF16)
            qh, kh, vh = _mla_proj(x, mod, pos, inv_lanes, w_in_cat, row(o_q_norm[j]),
                                   row(o_kv_norm[j]), wq_cat, wkv_cat)
            att = _attention(qh, kh, vh, 64)
            x = _outproj_norm(x, mod, row(o_ln_g[j]), row(o_ln_b[j]), o_w_out[j].astype(BF16), (att,))
        mod = _ada_modulation(c, f_ada_w, f_ada_b, layer)
        x = _conv_ffn(x, mod, row(f_ln_g[layer]), row(f_ln_b[layer]), f_w_up[layer].astype(BF16),
                      f_conv[layer], f_w_down[layer].astype(BF16))
    return x
```

```python
import functools

import jax
import jax.numpy as jnp
from jax import lax
from jax.experimental import pallas as pl
from jax.experimental.pallas import tpu as pltpu

F32 = jnp.float32
BF16 = jnp.bfloat16
HIGHEST = lax.Precision.HIGHEST

D_MODEL = 1024
DEPTH = 2
POOL_WINDOWS = (2, 4, 8, 16)
POOL_GROUP = 128
POOL_WIDTH = 512
POOL_HALO = 16
MLSTM_HEADS = 4
MLSTM_HEAD_DIM = 128
MLSTM_WIDTH = 512
MLSTM_CONV = 4
CONV_HALO = 8
GATE_PAD = 128
MLA_HEADS = 8
MLA_NOPE = 128
MLA_ROPE = 64
MLA_V = 128
MLA_QK = MLA_NOPE + MLA_ROPE
MLA_Q_LORA = 512
MLA_KV_LORA = 256
ROPE_THETA = 10000.0
D_FF = 2816
FFN_CONV = 3
DEEPNORM_ALPHA = (2 * DEPTH) ** 0.25
LN_EPS = 1e-5
RMS_EPS = 1e-6
LOG2_E = 1.4426950408889634

VMEM_CAP_BYTES = 64 * 1024 * 1024
LANES = 128
SUBLANES = 8

SEQ_TILE = 512
MLSTM_CHUNK = 256
ATT_TILE = 512
ATT_Q_BLOCKS = 2
ATT_CHUNK = 64
FFN_TILE = 256
ADA_TILE = 512


def _params(semantics, vmem_mib):
    limit = vmem_mib * 1024 * 1024
    assert limit <= VMEM_CAP_BYTES
    return pltpu.CompilerParams(dimension_semantics=semantics, vmem_limit_bytes=limit)


def _const_spec(shape):
    nd = len(shape)
    return pl.BlockSpec(shape, lambda *_: (0,) * nd, pipeline_mode=pl.Buffered(1))


def _layer_norm_rows(z, g, b):
    mu = jnp.mean(z, axis=-1, keepdims=True)
    zc = z - mu
    var = jnp.mean(zc * zc, axis=-1, keepdims=True)
    return zc * lax.rsqrt(var + LN_EPS) * g + b


def _sigmoid(x):
    return 1.0 / (1.0 + jnp.exp(-x))


def _ada_kernel(c_ref, w_ref, b_ref, o_ref):
    c = c_ref[...]
    cs = c * _sigmoid(c)
    o_ref[...] = jnp.dot(cs, w_ref[...], precision=HIGHEST, preferred_element_type=F32) + b_ref[...]


def _ada_modulation(c, w, b, layer):
    bsz, d = c.shape
    n_out = w.shape[-1]
    b3 = b.reshape(b.shape[0], 1, n_out)
    out = pl.pallas_call(
        _ada_kernel,
        grid=(n_out // ADA_TILE,),
        in_specs=[
            pl.BlockSpec((bsz, d), lambda n: (0, 0)),
            pl.BlockSpec((None, d, ADA_TILE), lambda n: (layer, 0, n)),
            pl.BlockSpec((None, 1, ADA_TILE), lambda n: (layer, 0, n)),
        ],
        out_specs=pl.BlockSpec((bsz, ADA_TILE), lambda n: (0, n)),
        out_shape=jax.ShapeDtypeStruct((bsz, n_out), F32),
        compiler_params=_params(("arbitrary",), 16),
        name="ada_modulation",
    )(c, w, b3)
    return out.reshape(bsz, 3, d)


def _even_proj_kernel(x_ref, mod_ref, w_ref, gb_ref, pw_ref, ps_ref, cw_ref,
                      ypool_ref, q_ref, k_ref, v_ref, o_ref, gcol_ref, grow_ref,
                      pbuf, qkbuf, *, ts):
    s = pl.program_id(1)

    @pl.when(s == 0)
    def _():
        pbuf[0:POOL_HALO, :] = jnp.zeros((POOL_HALO, POOL_WIDTH), F32)
        qkbuf[0:CONV_HALO, :] = jnp.zeros((CONV_HALO, 2 * MLSTM_WIDTH), F32)

    mod = mod_ref[0]
    shift, scale = mod[0:1], mod[1:2]
    h = (x_ref[0] * (1.0 + scale) + shift).astype(BF16)

    c0 = 0
    pbuf[POOL_HALO:POOL_HALO + ts, :] = jnp.dot(
        h, w_ref[:, c0:c0 + POOL_WIDTH], preferred_element_type=F32)
    c0 += POOL_WIDTH
    qkbuf[CONV_HALO:CONV_HALO + ts, :] = jnp.dot(
        h, w_ref[:, c0:c0 + 2 * MLSTM_WIDTH], preferred_element_type=F32)
    c0 += 2 * MLSTM_WIDTH
    v_ref[0] = jnp.dot(h, w_ref[:, c0:c0 + MLSTM_WIDTH], preferred_element_type=F32).astype(BF16)
    c0 += MLSTM_WIDTH
    o_ref[0] = _sigmoid(jnp.dot(h, w_ref[:, c0:c0 + MLSTM_WIDTH], preferred_element_type=F32))
    c0 += MLSTM_WIDTH
    gates = jnp.dot(h, w_ref[:, c0:c0 + GATE_PAD], preferred_element_type=F32) + gb_ref[...]

    lane = lax.broadcasted_iota(jnp.int32, gates.shape, 1)
    log_f = jnp.minimum(gates, 0.0) - jnp.log1p(jnp.exp(-jnp.abs(gates)))
    gates = jnp.where(lane >= MLSTM_HEADS, log_f, gates)
    gcol_ref[0] = gates
    grow_ref[0] = gates.T[0:2 * MLSTM_HEADS, :]

    t_glob = s * ts + lax.broadcasted_iota(jnp.int32, (ts, 1), 0)
    for gi, win in enumerate(POOL_WINDOWS):
        cols = slice(gi * POOL_GROUP, (gi + 1) * POOL_GROUP)
        cur = pbuf[POOL_HALO:POOL_HALO + ts, cols]
        acc = cur
        for j in range(1, win):
            acc = acc + pbuf[POOL_HALO - j:POOL_HALO - j + ts, cols]
        cnt = jnp.minimum(t_glob + 1, win).astype(F32)
        pooled = acc / cnt - cur
        mixed = jnp.dot(pooled.astype(BF16), pw_ref[gi], preferred_element_type=F32)
        ypool_ref[0, :, cols] = (mixed * ps_ref[:, cols]).astype(BF16)
    pbuf[0:POOL_HALO, :] = pbuf[ts:ts + POOL_HALO, :]

    cw = cw_ref[...]
    acc = qkbuf[CONV_HALO:CONV_HALO + ts, :] * cw[MLSTM_CONV - 1:MLSTM_CONV]
    for j in range(MLSTM_CONV - 1):
        off = CONV_HALO - (MLSTM_CONV - 1) + j
        acc = acc + qkbuf[off:off + ts, :] * cw[j:j + 1]
    qk = acc * _sigmoid(acc)
    q_ref[0] = qk[:, :MLSTM_WIDTH].astype(BF16)
    k_ref[0] = (qk[:, MLSTM_WIDTH:] * (MLSTM_HEAD_DIM ** -0.5)).astype(BF16)
    qkbuf[0:CONV_HALO, :] = qkbuf[ts:ts + CONV_HALO, :]


def _even_proj(x, mod, w_cat, gate_b, pool_w, pool_scale, conv_qk):
    bsz, seq, d = x.shape
    ts = SEQ_TILE
    n_cat = w_cat.shape[1]
    tok = lambda width: pl.BlockSpec((1, ts, width), lambda b, s: (b, s, 0))
    out_shape = (
        jax.ShapeDtypeStruct((bsz, seq, POOL_WIDTH), BF16),
        jax.ShapeDtypeStruct((bsz, seq, MLSTM_WIDTH), BF16),
        jax.ShapeDtypeStruct((bsz, seq, MLSTM_WIDTH), BF16),
        jax.ShapeDtypeStruct((bsz, seq, MLSTM_WIDTH), BF16),
        jax.ShapeDtypeStruct((bsz, seq, MLSTM_WIDTH), F32),
        jax.ShapeDtypeStruct((bsz, seq, GATE_PAD), F32),
        jax.ShapeDtypeStruct((bsz, 2 * MLSTM_HEADS, seq), F32),
    )
    return pl.pallas_call(
        functools.partial(_even_proj_kernel, ts=ts),
        grid=(bsz, seq // ts),
        in_specs=[
            tok(d),
            pl.BlockSpec((1, 3, d), lambda b, s: (b, 0, 0)),
            _const_spec((d, n_cat)),
            _const_spec((1, GATE_PAD)),
            _const_spec(pool_w.shape),
            _const_spec((1, POOL_WIDTH)),
            _const_spec(conv_qk.shape),
        ],
        out_specs=(
            tok(POOL_WIDTH), tok(MLSTM_WIDTH), tok(MLSTM_WIDTH), tok(MLSTM_WIDTH), tok(MLSTM_WIDTH),
            tok(GATE_PAD),
            pl.BlockSpec((1, 2 * MLSTM_HEADS, ts), lambda b, s: (b, 0, s)),
        ),
        out_shape=out_shape,
        scratch_shapes=[
            pltpu.VMEM((POOL_HALO + ts, POOL_WIDTH), F32),
            pltpu.VMEM((CONV_HALO + ts, 2 * MLSTM_WIDTH), F32),
        ],
        compiler_params=_params(("parallel", "arbitrary"), 48),
        name="even_proj",
    )(x, mod, w_cat, gate_b, pool_w, pool_scale, conv_qk)


def _mlstm_kernel(q_ref, k_ref, v_ref, o_ref, gc_ref, gr_ref, hn_ref, y_ref,
                  c_sc, n_sc, m_sc, *, chunk):
    s = pl.program_id(1)

    @pl.when(s == 0)
    def _():
        c_sc[...] = jnp.zeros(c_sc.shape, F32)
        n_sc[...] = jnp.zeros(n_sc.shape, F32)
        m_sc[...] = jnp.zeros(m_sc.shape, F32)

    gc = gc_ref[0]
    gr = gr_ref[0]
    row = lax.broadcasted_iota(jnp.int32, (chunk, chunk), 0)
    col = lax.broadcasted_iota(jnp.int32, (chunk, chunk), 1)
    causal = row >= col
    bc_col = jnp.dot(causal.astype(F32), gc, precision=HIGHEST, preferred_element_type=F32)
    bc_row = jnp.dot(gr, (row <= col).astype(F32), precision=HIGHEST, preferred_element_type=F32)

    nt = (((1,), (1,)), ((), ()))
    tn = (((0,), (0,)), ((), ()))
    for hd in range(MLSTM_HEADS):
        cols = slice(hd * MLSTM_HEAD_DIM, (hd + 1) * MLSTM_HEAD_DIM)
        i_col = gc[:, hd:hd + 1]
        b_col = bc_col[:, MLSTM_HEADS + hd:MLSTM_HEADS + hd + 1]
        i_row = gr[hd:hd + 1, :]
        b_row = bc_row[MLSTM_HEADS + hd:MLSTM_HEADS + hd + 1, :]
        m_prev = m_sc[hd][0:1, 0:1]
        c_prev = c_sc[hd]
        n_prev = n_sc[hd]
        qh = q_ref[0, :, cols]
        kh = k_ref[0, :, cols]
        vh = v_ref[0, :, cols]

        dmat = jnp.where(causal, b_col - b_row + i_row, -jnp.inf)
        m_inter = b_col + m_prev
        m_t = jnp.maximum(m_inter, jnp.max(dmat, axis=-1, keepdims=True))
        decay = jnp.exp(dmat - m_t)
        inter = jnp.exp(m_inter - m_t)
        scores = lax.dot_general(qh, kh, nt, preferred_element_type=F32) * decay
        num = (jnp.dot(scores.astype(BF16), vh, preferred_element_type=F32)
               + inter * jnp.dot(qh, c_prev.astype(BF16), preferred_element_type=F32))
        den = (jnp.sum(scores, axis=-1, keepdims=True)
               + inter * jnp.sum(qh.astype(F32) * n_prev, axis=-1, keepdims=True))
        hc = num / jnp.maximum(jnp.abs(den), jnp.exp(-m_t))

        mu = jnp.mean(hc, axis=-1, keepdims=True)
        hz = hc - mu
        var = jnp.mean(hz * hz, axis=-1, keepdims=True)
        hn = hz * lax.rsqrt(var + LN_EPS) * hn_ref[:, cols]
        y_ref[0, :, cols] = (o_ref[0, :, cols] * hn).astype(BF16)

        b_last = b_col[chunk - 1:chunk, :]
        g = b_last - b_col + i_col
        m_new = jnp.maximum(b_last + m_prev, jnp.max(g, axis=0, keepdims=True))
        wk = jnp.exp(g - m_new)
        carry = jnp.exp(b_last + m_prev - m_new)
        wv = (wk * vh.astype(F32)).astype(BF16)
        c_sc[hd] = carry * c_prev + lax.dot_general(kh, wv, tn, preferred_element_type=F32)
        n_sc[hd] = carry * n_prev + jnp.sum(wk * kh.astype(F32), axis=0, keepdims=True)
        m_sc[hd] = jnp.broadcast_to(m_new, (SUBLANES, LANES))


def _mlstm(q, k, v, o_sig, g_col, g_row, head_norm):
    bsz, seq, width = q.shape
    chunk = MLSTM_CHUNK
    tok = lambda w: pl.BlockSpec((1, chunk, w), lambda b, s: (b, s, 0))
    return pl.pallas_call(
        functools.partial(_mlstm_kernel, chunk=chunk),
        grid=(bsz, seq // chunk),
        in_specs=[
            tok(width), tok(width), tok(width), tok(width), tok(GATE_PAD),
            pl.BlockSpec((1, 2 * MLSTM_HEADS, chunk), lambda b, s: (b, 0, s)),
            _const_spec((1, width)),
        ],
        out_specs=tok(width),
        out_shape=jax.ShapeDtypeStruct((bsz, seq, width), BF16),
        scratch_shapes=[
            pltpu.VMEM((MLSTM_HEADS, MLSTM_HEAD_DIM, MLSTM_HEAD_DIM), F32),
            pltpu.VMEM((MLSTM_HEADS, 1, MLSTM_HEAD_DIM), F32),
            pltpu.VMEM((MLSTM_HEADS, SUBLANES, LANES), F32),
        ],
        compiler_params=_params(("parallel", "arbitrary"), 32),
        name="mlstm",
    )(q, k, v, o_sig, g_col, g_row, head_norm)


def _outproj_norm_kernel(*refs, widths):
    x_ref, mod_ref, g_ref, b_ref, w_ref = refs[:5]
    y_refs = refs[5:5 + len(widths)]
    out_ref = refs[5 + len(widths)]
    y = None
    r0 = 0
    for y_ref, width in zip(y_refs, widths):
        part = jnp.dot(y_ref[0], w_ref[r0:r0 + width, :], preferred_element_type=F32)
        y = part if y is None else y + part
        r0 += width
    gate = mod_ref[0][2:3]
    z = DEEPNORM_ALPHA * x_ref[0] + gate * y
    out_ref[0] = _layer_norm_rows(z, g_ref[...], b_ref[...])


def _outproj_norm(x, mod, ln_g, ln_b, w_out, ys):
    bsz, seq, d = x.shape
    ts = SEQ_TILE
    widths = tuple(y.shape[-1] for y in ys)
    tok = lambda w: pl.BlockSpec((1, ts, w), lambda b, s: (b, s, 0))
    return pl.pallas_call(
        functools.partial(_outproj_norm_kernel, widths=widths),
        grid=(bsz, seq // ts),
        in_specs=[
            tok(d),
            pl.BlockSpec((1, 3, d), lambda b, s: (b, 0, 0)),
            _const_spec((1, d)), _const_spec((1, d)),
            _const_spec(w_out.shape),
        ] + [tok(w) for w in widths],
        out_specs=tok(d),
        out_shape=jax.ShapeDtypeStruct((bsz, seq, d), F32),
        compiler_params=_params(("parallel", "arbitrary"), 32),
        name="outproj_norm",
    )(x, mod, ln_g, ln_b, w_out, *ys)


def _mla_proj_kernel(x_ref, mod_ref, pos_ref, inv_ref, w_in_ref, qn_ref, kvn_ref, wq_ref, wk_ref, wvt_ref,
                     q_out, k_out, vt_out, *, ts):
    mod = mod_ref[0]
    shift, scale = mod[0:1], mod[1:2]
    h = (x_ref[0] * (1.0 + scale) + shift).astype(BF16)
    u = jnp.dot(h, w_in_ref[...], preferred_element_type=F32)

    def rms(a, g):
        return a * lax.rsqrt(jnp.mean(a * a, axis=-1, keepdims=True) + RMS_EPS) * g

    cq = rms(u[:, :MLA_Q_LORA], qn_ref[...]).astype(BF16)
    ckv = rms(u[:, MLA_Q_LORA:MLA_Q_LORA + MLA_KV_LORA], kvn_ref[...]).astype(BF16)
    kr = u[:, MLA_Q_LORA + MLA_KV_LORA:]

    pos = jnp.broadcast_to(pos_ref[0], (LANES, ts)).T
    ang = pos * inv_ref[...]
    lane = lax.broadcasted_iota(jnp.int32, (1, LANES), 1)
    sign = jnp.where(lane % MLA_ROPE < MLA_ROPE // 2, -1.0, 1.0)
    cos = jnp.cos(ang)
    sin = jnp.sin(ang) * sign

    sm_scale = MLA_QK ** -0.5 * LOG2_E
    n_nope = MLA_HEADS * MLA_NOPE
    n_rope = MLA_HEADS * MLA_ROPE
    qa = jnp.dot(cq, wq_ref[...], preferred_element_type=F32) * sm_scale
    reps = n_rope // LANES
    q_rope = (qa[:, n_nope:n_nope + n_rope] * jnp.concatenate([cos] * reps, axis=1)
              + qa[:, n_nope + n_rope:] * jnp.concatenate([sin] * reps, axis=1))
    k_rope = (kr[:, :MLA_ROPE] * cos[:, :MLA_ROPE] + kr[:, MLA_ROPE:] * sin[:, :MLA_ROPE]).astype(BF16)
    k_nope = jnp.dot(ckv, wk_ref[...], preferred_element_type=F32)
    nt = (((1,), (1,)), ((), ()))
    vt = lax.dot_general(wvt_ref[...], ckv, nt, preferred_element_type=F32)

    for hd in range(MLA_HEADS):
        q_out[0, hd, :, 0:MLA_NOPE] = qa[:, hd * MLA_NOPE:(hd + 1) * MLA_NOPE].astype(BF16)
        q_out[0, hd, :, MLA_NOPE:MLA_QK] = q_rope[:, hd * MLA_ROPE:(hd + 1) * MLA_ROPE].astype(BF16)
        k_out[0, hd, :, 0:MLA_NOPE] = k_nope[:, hd * MLA_NOPE:(hd + 1) * MLA_NOPE].astype(BF16)
        k_out[0, hd, :, MLA_NOPE:MLA_QK] = k_rope
        vt_out[0, hd, 0] = vt[hd * MLA_V:(hd + 1) * MLA_V, :].astype(BF16)


def _mla_proj(x, mod, pos, inv, w_in, q_norm, kv_norm, w_uq, w_uk, w_uvt):
    bsz, seq, d = x.shape
    ts = SEQ_TILE
    assert ts == ATT_TILE
    head_spec = lambda w: pl.BlockSpec((1, MLA_HEADS, ts, w), lambda b, s: (b, 0, s, 0))
    return pl.pallas_call(
        functools.partial(_mla_proj_kernel, ts=ts),
        grid=(bsz, seq // ts),
        in_specs=[
            pl.BlockSpec((1, ts, d), lambda b, s: (b, s, 0)),
            pl.BlockSpec((1, 3, d), lambda b, s: (b, 0, 0)),
            pl.BlockSpec((1, 1, ts), lambda b, s: (b, 0, s)),
            _const_spec((1, LANES)),
            _const_spec(w_in.shape),
            _const_spec((1, MLA_Q_LORA)), _const_spec((1, MLA_KV_LORA)),
            _const_spec(w_uq.shape), _const_spec(w_uk.shape), _const_spec(w_uvt.shape),
        ],
        out_specs=(
            head_spec(MLA_QK), head_spec(MLA_QK),
            pl.BlockSpec((1, MLA_HEADS, 1, MLA_V, ts), lambda b, s: (b, 0, s, 0, 0)),
        ),
        out_shape=(
            jax.ShapeDtypeStruct((bsz, MLA_HEADS, seq, MLA_QK), BF16),
            jax.ShapeDtypeStruct((bsz, MLA_HEADS, seq, MLA_QK), BF16),
            jax.ShapeDtypeStruct((bsz, MLA_HEADS, seq // ts, MLA_V, ts), BF16),
        ),
        compiler_params=_params(("parallel", "arbitrary"), 48),
        name="mla_proj",
    )(x, mod, pos, inv, w_in, q_norm, kv_norm, w_uq, w_uk, w_uvt)


def _tree_reduce(op, a):
    while a.shape[0] > SUBLANES:
        half = a.shape[0] // 2
        a = op(a[:half], a[half:])
    return a


def _attn_kernel(q_ref, k_ref, vt_ref, o_ref, m_sc, l_sc, acc_sc, *, tile, q_blocks):
    qi = pl.program_id(2)
    m_sc[...] = jnp.full(m_sc.shape, -jnp.inf, F32)
    l_sc[...] = jnp.zeros(l_sc.shape, F32)
    acc_sc[...] = jnp.zeros(acc_sc.shape, F32)
    nt = (((1,), (1,)), ((), ()))

    def run_chains(chains):
        scores = []
        for j, qb, _ in chains:
            k = k_ref[0, 0, pl.ds(pl.multiple_of(j * tile, tile), tile), :]
            q = q_ref[0, 0, qb * tile:(qb + 1) * tile, :]
            scores.append(lax.dot_general(k, q, nt, preferred_element_type=F32))
        for (j, qb, masked), sc in zip(chains, scores):
            if masked:
                kc = lax.broadcasted_iota(jnp.int32, sc.shape, 0) // ATT_CHUNK
                qc = lax.broadcasted_iota(jnp.int32, sc.shape, 1) // ATT_CHUNK
                sc = jnp.where(qc >= kc, sc, -jnp.inf)
            m_prev = m_sc[qb]
            m_new = jnp.maximum(m_prev, jnp.max(_tree_reduce(jnp.maximum, sc), axis=0, keepdims=True))
            alpha = jnp.exp2(m_prev - m_new)
            p = jnp.exp2(sc - m_new)
            l_sc[qb] = alpha * l_sc[qb] + jnp.sum(_tree_reduce(jnp.add, p), axis=0, keepdims=True)
            acc_sc[qb] = alpha * acc_sc[qb] + jnp.dot(vt_ref[0, 0, j], p.astype(BF16),
                                                      preferred_element_type=F32)
            m_sc[qb] = m_new

    def body(i, carry):
        j0 = i * q_blocks
        run_chains([(j0 + dj, qb, False) for dj in range(q_blocks) for qb in range(q_blocks)])
        return carry

    lax.fori_loop(0, qi, body, 0)
    j0 = qi * q_blocks
    run_chains([(j0 + dj, qb, dj == qb) for dj in range(q_blocks) for qb in range(dj, q_blocks)])
    for qb in range(q_blocks):
        o_ref[0, qb * tile:(qb + 1) * tile, :] = (acc_sc[qb] / l_sc[qb]).T.astype(BF16)


def _attention(q, k, vt):
    bsz, heads, seq, dqk = q.shape
    n_kv, dv, tile = vt.shape[2:]
    q_blocks = ATT_Q_BLOCKS
    tq = tile * q_blocks
    return pl.pallas_call(
        functools.partial(_attn_kernel, tile=tile, q_blocks=q_blocks),
        grid=(bsz, heads, seq // tq),
        in_specs=[
            pl.BlockSpec((1, 1, tq, dqk), lambda b, h, i: (b, h, i, 0)),
            pl.BlockSpec((1, 1, seq, dqk), lambda b, h, i: (b, h, 0, 0)),
            pl.BlockSpec((1, 1, n_kv, dv, tile), lambda b, h, i: (b, h, 0, 0, 0)),
        ],
        out_specs=pl.BlockSpec((1, tq, dv), lambda b, h, i: (b, i, h)),
        out_shape=jax.ShapeDtypeStruct((bsz, seq, heads * dv), BF16),
        scratch_shapes=[
            pltpu.VMEM((q_blocks, 1, tile), F32),
            pltpu.VMEM((q_blocks, 1, tile), F32),
            pltpu.VMEM((q_blocks, dv, tile), F32),
        ],
        compiler_params=_params(("parallel", "parallel", "arbitrary"), 40),
        name="mla_attention",
    )(q, k, vt)


def _ffn_kernel(x_ref, mod_ref, g_ref, b_ref, wup_ref, cw_ref, wdn_ref, out_ref, halo, *, ts, tf):
    s = pl.program_id(1)

    @pl.when(s == 0)
    def _():
        halo[...] = jnp.zeros(halo.shape, F32)

    mod = mod_ref[0]
    shift, scale, gate = mod[0:1], mod[1:2], mod[2:3]
    x = x_ref[0]
    h = (x * (1.0 + scale) + shift).astype(BF16)
    rowid = lax.broadcasted_iota(jnp.int32, (ts, tf), 0)
    sqrt_half = 0.5 ** 0.5
    y = None
    for f in range(D_FF // tf):
        cols = slice(f * tf, (f + 1) * tf)
        gcols = slice(D_FF + f * tf, D_FF + (f + 1) * tf)
        a = jnp.dot(h, wup_ref[:, cols], preferred_element_type=F32)
        g = jnp.dot(h, wup_ref[:, gcols], preferred_element_type=F32)
        prev = halo[:, cols]
        halo[:, cols] = g[ts - CONV_HALO:ts, :]
        p1 = prev[CONV_HALO - 1:CONV_HALO]
        p2 = prev[CONV_HALO - 2:CONV_HALO - 1]
        g1 = jnp.where(rowid == 0, p1, pltpu.roll(g, 1, axis=0))
        g2 = jnp.where(rowid == 0, p2, jnp.where(rowid == 1, p1, pltpu.roll(g, 2, axis=0)))
        cw = cw_ref[:, cols]
        gc = cw[0:1] * g2 + cw[1:2] * g1 + cw[2:3] * g
        act = a * (0.5 * gc * (1.0 + lax.erf(gc * sqrt_half)))
        part = jnp.dot(act.astype(BF16), wdn_ref[cols, :], preferred_element_type=F32)
        y = part if y is None else y + part
    z = DEEPNORM_ALPHA * x + gate * y
    out_ref[0] = _layer_norm_rows(z, g_ref[...], b_ref[...])


def _conv_ffn(x, mod, ln_g, ln_b, w_up, conv_w, w_down):
    bsz, seq, d = x.shape
    ts, tf = SEQ_TILE, FFN_TILE
    assert D_FF % tf == 0
    tok = pl.BlockSpec((1, ts, d), lambda b, s: (b, s, 0))
    return pl.pallas_call(
        functools.partial(_ffn_kernel, ts=ts, tf=tf),
        grid=(bsz, seq // ts),
        in_specs=[
            tok,
            pl.BlockSpec((1, 3, d), lambda b, s: (b, 0, 0)),
            _const_spec((1, d)), _const_spec((1, d)),
            _const_spec(w_up.shape), _const_spec(conv_w.shape), _const_spec(w_down.shape),
        ],
        out_specs=tok,
        out_shape=jax.ShapeDtypeStruct((bsz, seq, d), F32),
        scratch_shapes=[pltpu.VMEM((CONV_HALO, D_FF), F32)],
        compiler_params=_params(("parallel", "arbitrary"), 56),
        name="conv_ffn",
    )(x, mod, ln_g, ln_b, w_up, conv_w, w_down)


def _swap_halves(w):
    half = w.shape[-1] // 2
    return jnp.concatenate([w[..., half:], w[..., :half]], axis=-1)


def kernel(x, c, positions, e_ada_w, e_ada_b, e_w_in, e_pool_w, e_pool_scale, e_conv_qk, e_gate_b, e_head_norm, e_w_out, e_ln_g, e_ln_b, o_ada_w, o_ada_b, o_w_in, o_q_norm, o_kv_norm, o_w_uq, o_w_ukv, o_w_out, o_ln_g, o_ln_b, f_ada_w, f_ada_b, f_w_up, f_conv, f_w_down, f_ln_g, f_ln_b):
    bsz, seq, d = x.shape
    row = lambda a: a.reshape(1, -1)

    half = MLA_ROPE // 2
    inv = ROPE_THETA ** (-jnp.arange(half, dtype=F32) / half)
    inv_lanes = jnp.tile(inv, LANES // half).reshape(1, LANES)
    pos = positions.astype(F32).reshape(bsz, 1, seq)

    for layer in range(DEPTH):
        j = layer // 2
        if layer % 2 == 0:
            mod = _ada_modulation(c, e_ada_w, e_ada_b, j)
            n_main = POOL_WIDTH + 4 * MLSTM_WIDTH
            w_in = e_w_in[j]
            w_cat = jnp.pad(w_in, ((0, 0), (0, GATE_PAD - (w_in.shape[1] - n_main)))).astype(BF16)
            gate_b = jnp.pad(e_gate_b[j], (0, GATE_PAD - 2 * MLSTM_HEADS)).reshape(1, GATE_PAD)
            y_pool, q, k, v, o_sig, g_col, g_row = _even_proj(
                x, mod, w_cat, gate_b, e_pool_w[j].astype(BF16), row(e_pool_scale[j]), e_conv_qk[j])
            y_mlstm = _mlstm(q, k, v, o_sig, g_col, g_row, row(e_head_norm[j]))
            x = _outproj_norm(x, mod, row(e_ln_g[j]), row(e_ln_b[j]), e_w_out[j].astype(BF16),
                              (y_pool, y_mlstm))
        else:
            mod = _ada_modulation(c, o_ada_w, o_ada_b, j)
            w_in = o_w_in[j]
            k_r = w_in[:, MLA_Q_LORA + MLA_KV_LORA:]
            w_in_cat = jnp.concatenate([w_in, _swap_halves(k_r)], axis=1).astype(BF16)
            wq = o_w_uq[j].reshape(MLA_Q_LORA, MLA_HEADS, MLA_QK)
            wq_rope = wq[:, :, MLA_NOPE:]
            wq_cat = jnp.concatenate([
                wq[:, :, :MLA_NOPE].reshape(MLA_Q_LORA, -1),
                wq_rope.reshape(MLA_Q_LORA, -1),
                _swap_halves(wq_rope).reshape(MLA_Q_LORA, -1)], axis=1).astype(BF16)
            wkv = o_w_ukv[j].reshape(MLA_KV_LORA, MLA_HEADS, MLA_NOPE + MLA_V)
            wk = wkv[:, :, :MLA_NOPE].reshape(MLA_KV_LORA, -1).astype(BF16)
            wvt = wkv[:, :, MLA_NOPE:].reshape(MLA_KV_LORA, -1).T.astype(BF16)
            qh, kh, vth = _mla_proj(x, mod, pos, inv_lanes, w_in_cat, row(o_q_norm[j]),
                                    row(o_kv_norm[j]), wq_cat, wk, wvt)
            att = _attention(qh, kh, vth)
            x = _outproj_norm(x, mod, row(o_ln_g[j]), row(o_ln_b[j]), o_w_out[j].astype(BF16), (att,))
        mod = _ada_modulation(c, f_ada_w, f_ada_b, layer)
        x = _conv_ffn(x, mod, row(f_ln_g[layer]), row(f_ln_b[layer]), f_w_up[layer].astype(BF16),
                      f_conv[layer], f_w_down[layer].astype(BF16))
    return x
```

```python
import functools

import jax
import jax.numpy as jnp
from jax import lax
from jax.experimental import pallas as pl
from jax.experimental.pallas import tpu as pltpu

F32 = jnp.float32
BF16 = jnp.bfloat16
HIGHEST = lax.Precision.HIGHEST

D_MODEL = 1024
DEPTH = 2
POOL_WINDOWS = (2, 4, 8, 16)
POOL_GROUP = 128
POOL_WIDTH = 512
POOL_HALO = 16
MLSTM_HEADS = 4
MLSTM_HEAD_DIM = 128
MLSTM_WIDTH = 512
MLSTM_CONV = 4
CONV_HALO = 8
GATE_PAD = 128
MLA_HEADS = 8
MLA_NOPE = 128
MLA_ROPE = 64
MLA_V = 128
MLA_QK = MLA_NOPE + MLA_ROPE
MLA_Q_LORA = 512
MLA_KV_LORA = 256
ROPE_THETA = 10000.0
D_FF = 2816
FFN_CONV = 3
DEEPNORM_ALPHA = (2 * DEPTH) ** 0.25
LN_EPS = 1e-5
RMS_EPS = 1e-6
LOG2_E = 1.4426950408889634

VMEM_CAP_BYTES = 64 * 1024 * 1024
LANES = 128
SUBLANES = 8
REDUCE_CHAINS = 4

SEQ_TILE = 512
MLSTM_CHUNK = 256
ATT_TILE = 512
MLA_ROW_BLOCKS = 2
ATT_Q_BLOCKS = 4
ATT_LOOKAHEAD = 3
ATT_CHUNK = 64
ATT_STALE_MARGIN = 64.0
FFN_TILE = 256
ADA_TILE = 512


def _params(semantics, vmem_mib):
    limit = vmem_mib * 1024 * 1024
    assert limit <= VMEM_CAP_BYTES
    return pltpu.CompilerParams(dimension_semantics=semantics, vmem_limit_bytes=limit)


def _const_spec(shape):
    nd = len(shape)
    return pl.BlockSpec(shape, lambda *_: (0,) * nd, pipeline_mode=pl.Buffered(1))


def _layer_norm_rows(z, g, b):
    mu = jnp.mean(z, axis=-1, keepdims=True)
    zc = z - mu
    var = jnp.mean(zc * zc, axis=-1, keepdims=True)
    return zc * lax.rsqrt(var + LN_EPS) * g + b


def _sigmoid(x):
    return 1.0 / (1.0 + jnp.exp(-x))


def _tree_reduce(op, a):
    slab = SUBLANES * REDUCE_CHAINS
    if a.shape[0] > slab and a.shape[0] % slab == 0:
        acc = a[:slab]
        for r0 in range(slab, a.shape[0], slab):
            acc = op(acc, a[r0:r0 + slab])
        a = acc
    while a.shape[0] > SUBLANES:
        half = a.shape[0] // 2
        a = op(a[:half], a[half:])
    return a


def _ada_kernel(c_ref, w_ref, b_ref, o_ref):
    c = c_ref[...]
    cs = c * _sigmoid(c)
    o_ref[...] = jnp.dot(cs, w_ref[...], precision=HIGHEST, preferred_element_type=F32) + b_ref[...]


def _ada_modulation(c, w, b, layer):
    bsz, d = c.shape
    n_out = w.shape[-1]
    b3 = b.reshape(b.shape[0], 1, n_out)
    out = pl.pallas_call(
        _ada_kernel,
        grid=(n_out // ADA_TILE,),
        in_specs=[
            pl.BlockSpec((bsz, d), lambda n: (0, 0)),
            pl.BlockSpec((None, d, ADA_TILE), lambda n: (layer, 0, n)),
            pl.BlockSpec((None, 1, ADA_TILE), lambda n: (layer, 0, n)),
        ],
        out_specs=pl.BlockSpec((bsz, ADA_TILE), lambda n: (0, n)),
        out_shape=jax.ShapeDtypeStruct((bsz, n_out), F32),
        compiler_params=_params(("arbitrary",), 16),
        name="ada_modulation",
    )(c, w, b3)
    return out.reshape(bsz, 3, d)


def _even_proj_kernel(x_ref, mod_ref, w_ref, wvt_ref, gb_ref, pw_ref, ps_ref, cw_ref,
                      ypool_ref, q_ref, k_ref, vt_ref, o_ref, gcol_ref, grow_ref,
                      pbuf, qkbuf, *, ts):
    s = pl.program_id(1)

    @pl.when(s == 0)
    def _():
        pbuf[0:POOL_HALO, :] = jnp.zeros((POOL_HALO, POOL_WIDTH), F32)
        qkbuf[0:CONV_HALO, :] = jnp.zeros((CONV_HALO, 2 * MLSTM_WIDTH), F32)

    mod = mod_ref[0]
    shift, scale = mod[0:1], mod[1:2]
    h = (x_ref[0] * (1.0 + scale) + shift).astype(BF16)

    c0 = 0
    pbuf[POOL_HALO:POOL_HALO + ts, :] = jnp.dot(
        h, w_ref[:, c0:c0 + POOL_WIDTH], preferred_element_type=F32)
    c0 += POOL_WIDTH
    qkbuf[CONV_HALO:CONV_HALO + ts, :] = jnp.dot(
        h, w_ref[:, c0:c0 + 2 * MLSTM_WIDTH], preferred_element_type=F32)
    c0 += 2 * MLSTM_WIDTH
    nt = (((1,), (1,)), ((), ()))
    vt_ref[0] = lax.dot_general(wvt_ref[...], h, nt, preferred_element_type=F32).astype(BF16)
    o_ref[0] = _sigmoid(jnp.dot(h, w_ref[:, c0:c0 + MLSTM_WIDTH], preferred_element_type=F32))
    c0 += MLSTM_WIDTH
    gates = jnp.dot(h, w_ref[:, c0:c0 + GATE_PAD], preferred_element_type=F32) + gb_ref[...]

    lane = lax.broadcasted_iota(jnp.int32, gates.shape, 1)
    log_f = jnp.minimum(gates, 0.0) - jnp.log1p(jnp.exp(-jnp.abs(gates)))
    gates = jnp.where(lane >= MLSTM_HEADS, log_f, gates)
    gcol_ref[0] = gates
    grow_ref[0] = gates.T[0:2 * MLSTM_HEADS, :]

    t_glob = s * ts + lax.broadcasted_iota(jnp.int32, (ts, 1), 0)
    for gi, win in enumerate(POOL_WINDOWS):
        cols = slice(gi * POOL_GROUP, (gi + 1) * POOL_GROUP)
        cur = pbuf[POOL_HALO:POOL_HALO + ts, cols]
        acc = cur
        for j in range(1, win):
            acc = acc + pbuf[POOL_HALO - j:POOL_HALO - j + ts, cols]
        cnt = jnp.minimum(t_glob + 1, win).astype(F32)
        pooled = acc / cnt - cur
        mixed = jnp.dot(pooled.astype(BF16), pw_ref[gi], preferred_element_type=F32)
        ypool_ref[0, :, cols] = (mixed * ps_ref[:, cols]).astype(BF16)
    pbuf[0:POOL_HALO, :] = pbuf[ts:ts + POOL_HALO, :]

    cw = cw_ref[...]
    acc = qkbuf[CONV_HALO:CONV_HALO + ts, :] * cw[MLSTM_CONV - 1:MLSTM_CONV]
    for j in range(MLSTM_CONV - 1):
        off = CONV_HALO - (MLSTM_CONV - 1) + j
        acc = acc + qkbuf[off:off + ts, :] * cw[j:j + 1]
    qk = acc * _sigmoid(acc)
    q_ref[0] = qk[:, :MLSTM_WIDTH].astype(BF16)
    k_ref[0] = (qk[:, MLSTM_WIDTH:] * (MLSTM_HEAD_DIM ** -0.5)).astype(BF16)
    qkbuf[0:CONV_HALO, :] = qkbuf[ts:ts + CONV_HALO, :]


def _even_proj(x, mod, w_cat, w_vt, gate_b, pool_w, pool_scale, conv_qk):
    bsz, seq, d = x.shape
    ts = SEQ_TILE
    n_cat = w_cat.shape[1]
    tok = lambda width: pl.BlockSpec((1, ts, width), lambda b, s: (b, s, 0))
    out_shape = (
        jax.ShapeDtypeStruct((bsz, seq, POOL_WIDTH), BF16),
        jax.ShapeDtypeStruct((bsz, seq, MLSTM_WIDTH), BF16),
        jax.ShapeDtypeStruct((bsz, seq, MLSTM_WIDTH), BF16),
        jax.ShapeDtypeStruct((bsz, MLSTM_WIDTH, seq), BF16),
        jax.ShapeDtypeStruct((bsz, seq, MLSTM_WIDTH), F32),
        jax.ShapeDtypeStruct((bsz, seq, GATE_PAD), F32),
        jax.ShapeDtypeStruct((bsz, 2 * MLSTM_HEADS, seq), F32),
    )
    return pl.pallas_call(
        functools.partial(_even_proj_kernel, ts=ts),
        grid=(bsz, seq // ts),
        in_specs=[
            tok(d),
            pl.BlockSpec((1, 3, d), lambda b, s: (b, 0, 0)),
            _const_spec((d, n_cat)),
            _const_spec(w_vt.shape),
            _const_spec((1, GATE_PAD)),
            _const_spec(pool_w.shape),
            _const_spec((1, POOL_WIDTH)),
            _const_spec(conv_qk.shape),
        ],
        out_specs=(
            tok(POOL_WIDTH), tok(MLSTM_WIDTH), tok(MLSTM_WIDTH),
            pl.BlockSpec((1, MLSTM_WIDTH, ts), lambda b, s: (b, 0, s)),
            tok(MLSTM_WIDTH), tok(GATE_PAD),
            pl.BlockSpec((1, 2 * MLSTM_HEADS, ts), lambda b, s: (b, 0, s)),
        ),
        out_shape=out_shape,
        scratch_shapes=[
            pltpu.VMEM((POOL_HALO + ts, POOL_WIDTH), F32),
            pltpu.VMEM((CONV_HALO + ts, 2 * MLSTM_WIDTH), F32),
        ],
        compiler_params=_params(("parallel", "arbitrary"), 48),
        name="even_proj",
    )(x, mod, w_cat, w_vt, gate_b, pool_w, pool_scale, conv_qk)


def _split_bf16(a):
    hi = a.astype(BF16)
    r1 = a - hi.astype(F32)
    mid = r1.astype(BF16)
    lo = (r1 - mid.astype(F32)).astype(BF16)
    return hi, mid, lo


def _mlstm_kernel(q_ref, k_ref, vt_ref, o_ref, gc_ref, gr_ref, hn_ref, y_ref, st_sc, m_sc, *, chunk):
    s = pl.program_id(1)

    @pl.when(s == 0)
    def _():
        st_sc[...] = jnp.zeros(st_sc.shape, F32)
        m_sc[...] = jnp.zeros(m_sc.shape, F32)

    gc = gc_ref[0]
    gr = gr_ref[0]
    key = lax.broadcasted_iota(jnp.int32, (chunk, chunk), 0)
    qry = lax.broadcasted_iota(jnp.int32, (chunk, chunk), 1)
    visible = key <= qry
    lower = (qry <= key).astype(BF16)
    upper = visible.astype(BF16)
    bc_col = sum(jnp.dot(lower, part, preferred_element_type=F32) for part in _split_bf16(gc))
    bc_row = sum(jnp.dot(part, upper, preferred_element_type=F32) for part in _split_bf16(gr))

    nt = (((1,), (1,)), ((), ()))
    dh = MLSTM_HEAD_DIM
    for hd in range(MLSTM_HEADS):
        cols = slice(hd * dh, (hd + 1) * dh)
        c_col = gc[:, hd:hd + 1] - bc_col[:, MLSTM_HEADS + hd:MLSTM_HEADS + hd + 1]
        i_row = gr[hd:hd + 1, :]
        b_row = bc_row[MLSTM_HEADS + hd:MLSTM_HEADS + hd + 1, :]
        m_prev = m_sc[hd][0:1, 0:1]
        state = st_sc[hd]
        qh = q_ref[0, :, cols]
        kh = k_ref[0, :, cols]
        vth = vt_ref[0, cols, :]

        dmat = jnp.where(visible, b_row + c_col, -jnp.inf)
        m_inter = b_row + m_prev
        m_t = jnp.maximum(m_inter, jnp.max(_tree_reduce(jnp.maximum, dmat), axis=0, keepdims=True))
        decay = jnp.exp(dmat - m_t)
        inter = jnp.exp(m_inter - m_t)
        scores = lax.dot_general(kh, qh, nt, preferred_element_type=F32) * decay
        sq = lax.dot_general(state.astype(BF16), qh, nt, preferred_element_type=F32)
        num = jnp.dot(vth, scores.astype(BF16), preferred_element_type=F32) + inter * sq[0:dh]
        den = jnp.sum(_tree_reduce(jnp.add, scores), axis=0, keepdims=True) + inter * sq[dh:dh + 1]
        hc = num / jnp.maximum(jnp.abs(den), jnp.exp(-m_t))

        mu = jnp.sum(_tree_reduce(jnp.add, hc), axis=0, keepdims=True) * (1.0 / dh)
        hz = hc - mu
        var = jnp.sum(_tree_reduce(jnp.add, hz * hz), axis=0, keepdims=True) * (1.0 / dh)
        hn = (hz * lax.rsqrt(var + LN_EPS)).T * hn_ref[:, cols]
        y_ref[0, :, cols] = (o_ref[0, :, cols] * hn).astype(BF16)

        b_last = b_row[:, chunk - 1:chunk]
        g = b_last - b_row + i_row
        m_new = jnp.maximum(b_last + m_prev, jnp.max(g, axis=1, keepdims=True))
        wk = jnp.exp(g - m_new)
        carry = jnp.exp(b_last + m_prev - m_new)
        lhs = jnp.concatenate([vth.astype(F32) * wk, jnp.broadcast_to(wk, (SUBLANES, chunk))], axis=0)
        st_sc[hd] = carry * state + jnp.dot(lhs.astype(BF16), kh, preferred_element_type=F32)
        m_sc[hd] = jnp.broadcast_to(m_new, (SUBLANES, LANES))


def _mlstm(q, k, vt, o_sig, g_col, g_row, head_norm):
    bsz, seq, width = q.shape
    chunk = MLSTM_CHUNK
    tok = lambda w: pl.BlockSpec((1, chunk, w), lambda b, s: (b, s, 0))
    rows = lambda r: pl.BlockSpec((1, r, chunk), lambda b, s: (b, 0, s))
    return pl.pallas_call(
        functools.partial(_mlstm_kernel, chunk=chunk),
        grid=(bsz, seq // chunk),
        in_specs=[
            tok(width), tok(width), rows(width), tok(width), tok(GATE_PAD), rows(2 * MLSTM_HEADS),
            _const_spec((1, width)),
        ],
        out_specs=tok(width),
        out_shape=jax.ShapeDtypeStruct((bsz, seq, width), BF16),
        scratch_shapes=[
            pltpu.VMEM((MLSTM_HEADS, MLSTM_HEAD_DIM + SUBLANES, MLSTM_HEAD_DIM), F32),
            pltpu.VMEM((MLSTM_HEADS, SUBLANES, LANES), F32),
        ],
        compiler_params=_params(("parallel", "arbitrary"), 32),
        name="mlstm",
    )(q, k, vt, o_sig, g_col, g_row, head_norm)


def _mla_proj_kernel(x_ref, mod_ref, pos_ref, inv_ref, w_in_ref, qn_ref, kvn_ref, wq_ref, wk_ref, wvt_ref,
                     q_out, k_out, vt_out, *, ts):
    mod = mod_ref[0]
    shift, scale = mod[0:1], mod[1:2]
    h = (x_ref[0] * (1.0 + scale) + shift).astype(BF16)

    ang = inv_ref[...] * pos_ref[0]
    cos_t = jnp.cos(ang)
    sin_t = jnp.sin(ang)
    pairs = LANES // MLA_ROPE
    cos_all = jnp.concatenate([cos_t, cos_t] * pairs, axis=0).T
    sin_all = jnp.concatenate([-sin_t, sin_t] * pairs, axis=0).T

    def rms(a, g):
        return a * lax.rsqrt(jnp.mean(a * a, axis=-1, keepdims=True) + RMS_EPS) * g

    sm_scale = MLA_QK ** -0.5 * LOG2_E
    n_nope = MLA_HEADS * MLA_NOPE
    n_rope = MLA_HEADS * MLA_ROPE
    reps = n_rope // LANES
    nt = (((1,), (1,)), ((), ()))

    rb = ts // MLA_ROW_BLOCKS
    blocks = [slice(i * rb, (i + 1) * rb) for i in range(MLA_ROW_BLOCKS)]
    us = [jnp.dot(h[r], w_in_ref[...], preferred_element_type=F32) for r in blocks]
    for r, u in zip(blocks, us):
        cq = rms(u[:, :MLA_Q_LORA], qn_ref[...]).astype(BF16)
        ckv = rms(u[:, MLA_Q_LORA:MLA_Q_LORA + MLA_KV_LORA], kvn_ref[...]).astype(BF16)
        kr = u[:, MLA_Q_LORA + MLA_KV_LORA:]
        cos, sin = cos_all[r], sin_all[r]
        qa = jnp.dot(cq, wq_ref[...], preferred_element_type=F32) * sm_scale
        q_rope = (qa[:, n_nope:n_nope + n_rope] * jnp.concatenate([cos] * reps, axis=1)
                  + qa[:, n_nope + n_rope:] * jnp.concatenate([sin] * reps, axis=1))
        k_rope = (kr[:, :MLA_ROPE] * cos[:, :MLA_ROPE]
                  + kr[:, MLA_ROPE:] * sin[:, :MLA_ROPE]).astype(BF16)
        k_nope = jnp.dot(ckv, wk_ref[...], preferred_element_type=F32)
        vt = lax.dot_general(wvt_ref[...], ckv, nt, preferred_element_type=F32)
        for hd in range(MLA_HEADS):
            q_out[0, hd, r, 0:MLA_NOPE] = qa[:, hd * MLA_NOPE:(hd + 1) * MLA_NOPE].astype(BF16)
            q_out[0, hd, r, MLA_NOPE:MLA_QK] = q_rope[:, hd * MLA_ROPE:(hd + 1) * MLA_ROPE].astype(BF16)
            k_out[0, hd, r, 0:MLA_NOPE] = k_nope[:, hd * MLA_NOPE:(hd + 1) * MLA_NOPE].astype(BF16)
            k_out[0, hd, r, MLA_NOPE:MLA_QK] = k_rope
            vt_out[0, hd, 0, :, r] = vt[hd * MLA_V:(hd + 1) * MLA_V, :].astype(BF16)


def _mla_proj(x, mod, pos, inv, w_in, q_norm, kv_norm, w_uq, w_uk, w_uvt):
    bsz, seq, d = x.shape
    ts = SEQ_TILE
    assert ts == ATT_TILE
    head_spec = lambda w: pl.BlockSpec((1, MLA_HEADS, ts, w), lambda b, s: (b, 0, s, 0))
    return pl.pallas_call(
        functools.partial(_mla_proj_kernel, ts=ts),
        grid=(bsz, seq // ts),
        in_specs=[
            pl.BlockSpec((1, ts, d), lambda b, s: (b, s, 0)),
            pl.BlockSpec((1, 3, d), lambda b, s: (b, 0, 0)),
            pl.BlockSpec((1, 1, ts), lambda b, s: (b, 0, s)),
            _const_spec(inv.shape),
            _const_spec(w_in.shape),
            _const_spec((1, MLA_Q_LORA)), _const_spec((1, MLA_KV_LORA)),
            _const_spec(w_uq.shape), _const_spec(w_uk.shape), _const_spec(w_uvt.shape),
        ],
        out_specs=(
            head_spec(MLA_QK), head_spec(MLA_QK),
            pl.BlockSpec((1, MLA_HEADS, 1, MLA_V, ts), lambda b, s: (b, 0, s, 0, 0)),
        ),
        out_shape=(
            jax.ShapeDtypeStruct((bsz, MLA_HEADS, seq, MLA_QK), BF16),
            jax.ShapeDtypeStruct((bsz, MLA_HEADS, seq, MLA_QK), BF16),
            jax.ShapeDtypeStruct((bsz, MLA_HEADS, seq // ts, MLA_V, ts), BF16),
        ),
        compiler_params=_params(("parallel", "arbitrary"), 48),
        name="mla_proj",
    )(x, mod, pos, inv, w_in, q_norm, kv_norm, w_uq, w_uk, w_uvt)


def _attn_kernel(q_ref, k_ref, vt_ref, o_ref, m_sc, l_sc, acc_sc, *, tile, q_blocks):
    qi = pl.program_id(2)
    nt = (((1,), (1,)), ((), ()))

    def score_tile(chain):
        j, qb, _ = chain
        k = k_ref[0, 0, pl.ds(pl.multiple_of(j * tile, tile), tile), :]
        q = q_ref[0, 0, qb * tile:(qb + 1) * tile, :]
        return lax.dot_general(k, q, nt, preferred_element_type=F32)

    def for_each_chain(chains, consume):
        scores = []
        for idx, chain in enumerate(chains):
            while len(scores) < min(len(chains), idx + 1 + ATT_LOOKAHEAD):
                scores.append(score_tile(chains[len(scores)]))
            consume(chain, scores[idx])

    def col_max(a):
        return jnp.max(_tree_reduce(jnp.maximum, a), axis=0, keepdims=True)

    def col_sum(a):
        return jnp.sum(_tree_reduce(jnp.add, a), axis=0, keepdims=True)

    def pv(j, p):
        return jnp.dot(vt_ref[0, 0, j], p.astype(BF16), preferred_element_type=F32)

    def load_state(side):
        return [(m_sc[side, qb], l_sc[side, qb], acc_sc[side, qb]) for qb in range(q_blocks)]

    def store_state(side, state):
        for qb, (m, l, acc) in enumerate(state):
            m_sc[side, qb] = m
            l_sc[side, qb] = l
            acc_sc[side, qb] = acc

    def exact_update(state, chains):
        state = list(state)

        def consume(chain, sc):
            j, qb, masked = chain
            if masked:
                kc = lax.broadcasted_iota(jnp.int32, sc.shape, 0) // ATT_CHUNK
                qc = lax.broadcasted_iota(jnp.int32, sc.shape, 1) // ATT_CHUNK
                sc = jnp.where(qc >= kc, sc, -jnp.inf)
            m_prev, l_prev, acc_prev = state[qb]
            m_new = jnp.maximum(m_prev, col_max(sc))
            alpha = jnp.exp2(m_prev - m_new)
            p = jnp.exp2(sc - m_new)
            state[qb] = (m_new, alpha * l_prev + col_sum(p), alpha * acc_prev + pv(j, p))

        for_each_chain(chains, consume)
        return state

    def stale_update(state, chains):
        m0 = [st[0] for st in state]
        l = [st[1] for st in state]
        acc = [st[2] for st in state]
        top = [None] * q_blocks

        def consume(chain, sc):
            j, qb, _ = chain
            p = jnp.exp2(sc - m0[qb])
            tile_max = col_max(sc)
            l[qb] = l[qb] + col_sum(p)
            acc[qb] = acc[qb] + pv(j, p)
            top[qb] = tile_max if top[qb] is None else jnp.maximum(top[qb], tile_max)

        for_each_chain(chains, consume)
        new_state = []
        worst = None
        for qb in range(q_blocks):
            m_new = jnp.maximum(m0[qb], top[qb])
            alpha = jnp.exp2(m0[qb] - m_new)
            new_state.append((m_new, alpha * l[qb], alpha * acc[qb]))
            excess = top[qb] - m0[qb]
            worst = excess if worst is None else jnp.maximum(worst, excess)
        return new_state, worst

    j0 = qi * q_blocks
    diag = [(j0 + dj, qb, dj == qb) for dj in range(q_blocks) for qb in range(dj, q_blocks)]
    empty = [(jnp.full((1, tile), -jnp.inf, F32), jnp.zeros((1, tile), F32),
              jnp.zeros((acc_sc.shape[2], tile), F32)) for _ in range(q_blocks)]
    store_state(0, exact_update(empty, diag))

    def body(i, carry):
        src = i % 2
        chains = [(i * q_blocks + dj, qb, False) for dj in range(q_blocks) for qb in range(q_blocks)]
        state, worst = stale_update(load_state(src), chains)
        store_state(1 - src, state)

        @pl.when(jnp.max(worst) > ATT_STALE_MARGIN)
        def _():
            store_state(1 - src, exact_update(load_state(src), chains))

        return carry

    lax.fori_loop(0, qi, body, 0)
    last = qi % 2
    for qb in range(q_blocks):
        o_ref[0, qb * tile:(qb + 1) * tile, :] = (acc_sc[last, qb] / l_sc[last, qb]).T.astype(BF16)


def _attention(q, k, vt):
    bsz, heads, seq, dqk = q.shape
    n_kv, dv, tile = vt.shape[2:]
    q_blocks = ATT_Q_BLOCKS
    tq = tile * q_blocks
    return pl.pallas_call(
        functools.partial(_attn_kernel, tile=tile, q_blocks=q_blocks),
        grid=(bsz, heads, seq // tq),
        in_specs=[
            pl.BlockSpec((1, 1, tq, dqk), lambda b, h, i: (b, h, i, 0)),
            pl.BlockSpec((1, 1, seq, dqk), lambda b, h, i: (b, h, 0, 0)),
            pl.BlockSpec((1, 1, n_kv, dv, tile), lambda b, h, i: (b, h, 0, 0, 0)),
        ],
        out_specs=pl.BlockSpec((1, tq, dv), lambda b, h, i: (b, i, h)),
        out_shape=jax.ShapeDtypeStruct((bsz, seq, heads * dv), BF16),
        scratch_shapes=[
            pltpu.VMEM((2, q_blocks, 1, tile), F32),
            pltpu.VMEM((2, q_blocks, 1, tile), F32),
            pltpu.VMEM((2, q_blocks, dv, tile), F32),
        ],
        compiler_params=_params(("parallel", "parallel", "arbitrary"), 40),
        name="mla_attention",
    )(q, k, vt)


def _mix_ffn_kernel(*refs, widths, ts, tf):
    (x_ref, mmod_ref, mg_ref, mb_ref, wo_ref,
     fmod_ref, fg_ref, fb_ref, wup_ref, cw_ref, wdn_ref) = refs[:11]
    y_refs = refs[11:11 + len(widths)]
    out_ref, halo = refs[11 + len(widths):]
    s = pl.program_id(1)

    @pl.when(s == 0)
    def _():
        halo[...] = jnp.zeros(halo.shape, F32)

    y = None
    r0 = 0
    for y_ref, width in zip(y_refs, widths):
        part = jnp.dot(y_ref[0], wo_ref[r0:r0 + width, :], preferred_element_type=F32)
        y = part if y is None else y + part
        r0 += width
    x = _layer_norm_rows(DEEPNORM_ALPHA * x_ref[0] + mmod_ref[0][2:3] * y, mg_ref[...], mb_ref[...])

    fmod = fmod_ref[0]
    shift, scale, gate = fmod[0:1], fmod[1:2], fmod[2:3]
    h = (x * (1.0 + scale) + shift).astype(BF16)
    rowid = lax.broadcasted_iota(jnp.int32, (ts, tf), 0)
    sqrt_half = 0.5 ** 0.5
    n_f = D_FF // tf

    def up(f):
        a = jnp.dot(h, wup_ref[:, f * tf:(f + 1) * tf], preferred_element_type=F32)
        g = jnp.dot(h, wup_ref[:, D_FF + f * tf:D_FF + (f + 1) * tf], preferred_element_type=F32)
        return a, g

    y = None
    nxt = up(0)
    for f in range(n_f):
        a, g = nxt
        if f + 1 < n_f:
            nxt = up(f + 1)
        cols = slice(f * tf, (f + 1) * tf)
        prev = halo[:, cols]
        halo[:, cols] = g[ts - CONV_HALO:ts, :]
        p1 = prev[CONV_HALO - 1:CONV_HALO]
        p2 = prev[CONV_HALO - 2:CONV_HALO - 1]
        g1 = jnp.where(rowid == 0, p1, pltpu.roll(g, 1, axis=0))
        g2 = jnp.where(rowid == 0, p2, jnp.where(rowid == 1, p1, pltpu.roll(g, 2, axis=0)))
        cw = cw_ref[:, cols]
        gc = cw[0:1] * g2 + cw[1:2] * g1 + cw[2:3] * g
        act = a * (0.5 * gc * (1.0 + lax.erf(gc * sqrt_half)))
        part = jnp.dot(act.astype(BF16), wdn_ref[cols, :], preferred_element_type=F32)
        y = part if y is None else y + part
    z = DEEPNORM_ALPHA * x + gate * y
    out_ref[0] = _layer_norm_rows(z, fg_ref[...], fb_ref[...])


def _mix_ffn(x, ys, mix_mod, mix_g, mix_b, w_out, ffn_mod, ffn_g, ffn_b, w_up, conv_w, w_down):
    bsz, seq, d = x.shape
    ts, tf = SEQ_TILE, FFN_TILE
    assert D_FF % tf == 0
    widths = tuple(y.shape[-1] for y in ys)
    tok = lambda w: pl.BlockSpec((1, ts, w), lambda b, s: (b, s, 0))
    mod_spec = pl.BlockSpec((1, 3, d), lambda b, s: (b, 0, 0))
    return pl.pallas_call(
        functools.partial(_mix_ffn_kernel, widths=widths, ts=ts, tf=tf),
        grid=(bsz, seq // ts),
        in_specs=[
            tok(d),
            mod_spec, _const_spec((1, d)), _const_spec((1, d)), _const_spec(w_out.shape),
            mod_spec, _const_spec((1, d)), _const_spec((1, d)),
            _const_spec(w_up.shape), _const_spec(conv_w.shape), _const_spec(w_down.shape),
        ] + [tok(w) for w in widths],
        out_specs=tok(d),
        out_shape=jax.ShapeDtypeStruct((bsz, seq, d), F32),
        scratch_shapes=[pltpu.VMEM((CONV_HALO, D_FF), F32)],
        compiler_params=_params(("parallel", "arbitrary"), 56),
        name="mix_ffn",
    )(x, mix_mod, mix_g, mix_b, w_out, ffn_mod, ffn_g, ffn_b, w_up, conv_w, w_down, *ys)


def _swap_halves(w):
    half = w.shape[-1] // 2
    return jnp.concatenate([w[..., half:], w[..., :half]], axis=-1)


def kernel(x, c, positions, e_ada_w, e_ada_b, e_w_in, e_pool_w, e_pool_scale, e_conv_qk, e_gate_b, e_head_norm, e_w_out, e_ln_g, e_ln_b, o_ada_w, o_ada_b, o_w_in, o_q_norm, o_kv_norm, o_w_uq, o_w_ukv, o_w_out, o_ln_g, o_ln_b, f_ada_w, f_ada_b, f_w_up, f_conv, f_w_down, f_ln_g, f_ln_b):
    bsz, seq, d = x.shape
    row = lambda a: a.reshape(1, -1)

    half = MLA_ROPE // 2
    inv = ROPE_THETA ** (-jnp.arange(half, dtype=F32) / half)
    inv_col = inv.reshape(half, 1)
    pos = positions.astype(F32).reshape(bsz, 1, seq)

    for layer in range(DEPTH):
        j = layer // 2
        if layer % 2 == 0:
            mod = _ada_modulation(c, e_ada_w, e_ada_b, j)
            w_in = e_w_in[j]
            v0 = POOL_WIDTH + 2 * MLSTM_WIDTH
            gate_pad = jnp.zeros((d, GATE_PAD - 2 * MLSTM_HEADS), F32)
            w_cat = jnp.concatenate(
                [w_in[:, :v0], w_in[:, v0 + MLSTM_WIDTH:], gate_pad], axis=1).astype(BF16)
            w_vt = w_in[:, v0:v0 + MLSTM_WIDTH].T.astype(BF16)
            gate_b = jnp.pad(e_gate_b[j], (0, GATE_PAD - 2 * MLSTM_HEADS)).reshape(1, GATE_PAD)
            y_pool, q, k, vt, o_sig, g_col, g_row = _even_proj(
                x, mod, w_cat, w_vt, gate_b, e_pool_w[j].astype(BF16), row(e_pool_scale[j]),
                e_conv_qk[j])
            y_mlstm = _mlstm(q, k, vt, o_sig, g_col, g_row, row(e_head_norm[j]))
            ys = (y_pool, y_mlstm)
            mix = (mod, row(e_ln_g[j]), row(e_ln_b[j]), e_w_out[j].astype(BF16))
        else:
            mod = _ada_modulation(c, o_ada_w, o_ada_b, j)
            w_in = o_w_in[j]
            k_r = w_in[:, MLA_Q_LORA + MLA_KV_LORA:]
            w_in_cat = jnp.concatenate([w_in, _swap_halves(k_r)], axis=1).astype(BF16)
            wq = o_w_uq[j].reshape(MLA_Q_LORA, MLA_HEADS, MLA_QK)
            wq_rope = wq[:, :, MLA_NOPE:]
            wq_cat = jnp.concatenate([
                wq[:, :, :MLA_NOPE].reshape(MLA_Q_LORA, -1),
                wq_rope.reshape(MLA_Q_LORA, -1),
                _swap_halves(wq_rope).reshape(MLA_Q_LORA, -1)], axis=1).astype(BF16)
            wkv = o_w_ukv[j].reshape(MLA_KV_LORA, MLA_HEADS, MLA_NOPE + MLA_V)
            wk = wkv[:, :, :MLA_NOPE].reshape(MLA_KV_LORA, -1).astype(BF16)
            wvt = wkv[:, :, MLA_NOPE:].reshape(MLA_KV_LORA, -1).T.astype(BF16)
            qh, kh, vth = _mla_proj(x, mod, pos, inv_col, w_in_cat, row(o_q_norm[j]),
                                    row(o_kv_norm[j]), wq_cat, wk, wvt)
            ys = (_attention(qh, kh, vth),)
            mix = (mod, row(o_ln_g[j]), row(o_ln_b[j]), o_w_out[j].astype(BF16))
        ffn_mod = _ada_modulation(c, f_ada_w, f_ada_b, layer)
        x = _mix_ffn(x, ys, *mix, ffn_mod, row(f_ln_g[layer]), row(f_ln_b[layer]),
                     f_w_up[layer].astype(BF16), f_conv[layer], f_w_down[layer].astype(BF16))
    return x
```

```python
import functools

import jax
import jax.numpy as jnp
from jax import lax
from jax.experimental import pallas as pl
from jax.experimental.pallas import tpu as pltpu

F32 = jnp.float32
BF16 = jnp.bfloat16
HIGHEST = lax.Precision.HIGHEST

D_MODEL = 1024
DEPTH = 2
POOL_WINDOWS = (2, 4, 8, 16)
POOL_GROUP = 128
POOL_WIDTH = 512
POOL_HALO = 16
MLSTM_HEADS = 4
MLSTM_HEAD_DIM = 128
MLSTM_WIDTH = 512
MLSTM_CONV = 4
CONV_HALO = 8
GATE_PAD = 128
MLA_HEADS = 8
MLA_NOPE = 128
MLA_ROPE = 64
MLA_V = 128
MLA_QK = MLA_NOPE + MLA_ROPE
MLA_Q_LORA = 512
MLA_KV_LORA = 256
ROPE_THETA = 10000.0
D_FF = 2816
FFN_CONV = 3
DEEPNORM_ALPHA = (2 * DEPTH) ** 0.25
LN_EPS = 1e-5
RMS_EPS = 1e-6
LOG2_E = 1.4426950408889634

VMEM_CAP_BYTES = 64 * 1024 * 1024
LANES = 128
SUBLANES = 8
REDUCE_CHAINS = 4

SEQ_TILE = 512
MLSTM_CHUNK = 256
ATT_TILE = 512
MLA_ROW_BLOCKS = 2
ATT_Q_BLOCKS = 4
ATT_LOOKAHEAD = 3
ATT_CHUNK = 64
ATT_STALE_MARGIN = 64.0
FFN_TILE = 256
FFN_SEQ_TILE = 1024
FFN_ROW_BLOCKS = 4
FFN_SIDE_SPLIT = 4
ADA_TILE = 512


def _params(semantics, vmem_mib):
    limit = vmem_mib * 1024 * 1024
    assert limit <= VMEM_CAP_BYTES
    return pltpu.CompilerParams(dimension_semantics=semantics, vmem_limit_bytes=limit)


def _const_spec(shape):
    nd = len(shape)
    return pl.BlockSpec(shape, lambda *_: (0,) * nd, pipeline_mode=pl.Buffered(1))


def _layer_norm_rows(z, g, b):
    mu = jnp.mean(z, axis=-1, keepdims=True)
    zc = z - mu
    var = jnp.mean(zc * zc, axis=-1, keepdims=True)
    return zc * lax.rsqrt(var + LN_EPS) * g + b


def _sigmoid(x):
    return 1.0 / (1.0 + jnp.exp(-x))


def _tree_reduce(op, a):
    slab = SUBLANES * REDUCE_CHAINS
    if a.shape[0] > slab and a.shape[0] % slab == 0:
        acc = a[:slab]
        for r0 in range(slab, a.shape[0], slab):
            acc = op(acc, a[r0:r0 + slab])
        a = acc
    while a.shape[0] > SUBLANES:
        half = a.shape[0] // 2
        a = op(a[:half], a[half:])
    return a


def _split_bf16(a):
    hi = a.astype(BF16)
    r1 = a - hi.astype(F32)
    mid = r1.astype(BF16)
    lo = (r1 - mid.astype(F32)).astype(BF16)
    return hi, mid, lo


def _ada_kernel(c_ref, w_ref, b_ref, o_ref):
    c = c_ref[...]
    cs = c * _sigmoid(c)
    o_ref[...] = jnp.dot(cs, w_ref[...], precision=HIGHEST, preferred_element_type=F32) + b_ref[...]


def _ada_modulation(c, w, b, layer):
    bsz, d = c.shape
    n_out = w.shape[-1]
    b3 = b.reshape(b.shape[0], 1, n_out)
    out = pl.pallas_call(
        _ada_kernel,
        grid=(n_out // ADA_TILE,),
        in_specs=[
            pl.BlockSpec((bsz, d), lambda n: (0, 0)),
            pl.BlockSpec((None, d, ADA_TILE), lambda n: (layer, 0, n)),
            pl.BlockSpec((None, 1, ADA_TILE), lambda n: (layer, 0, n)),
        ],
        out_specs=pl.BlockSpec((bsz, ADA_TILE), lambda n: (0, n)),
        out_shape=jax.ShapeDtypeStruct((bsz, n_out), F32),
        compiler_params=_params(("arbitrary",), 16),
        name="ada_modulation",
    )(c, w, b3)
    return out.reshape(bsz, 3, d)


def _even_mixer_kernel(x_ref, mod_ref, w_ref, wvt_ref, gb_ref, pw_ref, ps_ref, cw_ref, hn_ref,
                       ypool_ref, ymlstm_ref, pbuf, qkbuf, st_sc, m_sc, *, ts, chunk):
    s = pl.program_id(1)

    @pl.when(s == 0)
    def _():
        pbuf[0:POOL_HALO, :] = jnp.zeros((POOL_HALO, POOL_WIDTH), F32)
        qkbuf[0:CONV_HALO, :] = jnp.zeros((CONV_HALO, 2 * MLSTM_WIDTH), F32)
        st_sc[...] = jnp.zeros(st_sc.shape, F32)
        m_sc[...] = jnp.zeros(m_sc.shape, F32)

    mod = mod_ref[0]
    shift, scale = mod[0:1], mod[1:2]
    h = (x_ref[0] * (1.0 + scale) + shift).astype(BF16)

    nt = (((1,), (1,)), ((), ()))
    c_qk = POOL_WIDTH
    c_o = c_qk + 2 * MLSTM_WIDTH
    c_g = c_o + MLSTM_WIDTH
    dh = MLSTM_HEAD_DIM
    key = lax.broadcasted_iota(jnp.int32, (chunk, chunk), 0)
    qry = lax.broadcasted_iota(jnp.int32, (chunk, chunk), 1)
    visible = key <= qry
    lower = (qry <= key).astype(BF16)
    upper = visible.astype(BF16)
    cw = cw_ref[...]

    def project_steps(c, out):
        r0 = c * chunk
        hc = h[r0:r0 + chunk]

        def qk_step():
            qkbuf[CONV_HALO + r0:CONV_HALO + r0 + chunk, :] = jnp.dot(
                hc, w_ref[:, c_qk:c_o], preferred_element_type=F32)

        def gate_step():
            out["gates"] = (jnp.dot(hc, w_ref[:, c_g:c_g + GATE_PAD], preferred_element_type=F32)
                            + gb_ref[...])

        def v_step():
            out["vt"] = lax.dot_general(wvt_ref[...], hc, nt,
                                        preferred_element_type=F32).astype(BF16)

        def o_step():
            out["o_sig"] = _sigmoid(jnp.dot(hc, w_ref[:, c_o:c_g], preferred_element_type=F32))

        def pool_step():
            pbuf[POOL_HALO + r0:POOL_HALO + r0 + chunk, :] = jnp.dot(
                hc, w_ref[:, 0:c_qk], preferred_element_type=F32)

        return [qk_step, gate_step, v_step, o_step, pool_step]

    def mix_steps(c, inp):
        r0 = c * chunk
        rows = slice(r0, r0 + chunk)
        loc = {}

        def prep_step():
            gates = inp["gates"]
            lane = lax.broadcasted_iota(jnp.int32, gates.shape, 1)
            log_f = jnp.minimum(gates, 0.0) - jnp.log1p(jnp.exp(-jnp.abs(gates)))
            gc = jnp.where(lane >= MLSTM_HEADS, log_f, gates)
            gr = gc.T[0:2 * MLSTM_HEADS, :]
            acc = qkbuf[CONV_HALO + r0:CONV_HALO + r0 + chunk, :] * cw[MLSTM_CONV - 1:MLSTM_CONV]
            for j in range(MLSTM_CONV - 1):
                off = CONV_HALO - (MLSTM_CONV - 1) + j + r0
                acc = acc + qkbuf[off:off + chunk, :] * cw[j:j + 1]
            qk = acc * _sigmoid(acc)
            loc["q"] = qk[:, :MLSTM_WIDTH].astype(BF16)
            loc["k"] = (qk[:, MLSTM_WIDTH:] * (dh ** -0.5)).astype(BF16)
            loc["bc_col"] = sum(jnp.dot(lower, part, preferred_element_type=F32)
                                for part in _split_bf16(gc))
            loc["bc_row"] = sum(jnp.dot(part, upper, preferred_element_type=F32)
                                for part in _split_bf16(gr))
            loc["gc"], loc["gr"] = gc, gr

        def head_step(hd):
            gc, gr = loc["gc"], loc["gr"]
            cols = slice(hd * dh, (hd + 1) * dh)
            c_col = gc[:, hd:hd + 1] - loc["bc_col"][:, MLSTM_HEADS + hd:MLSTM_HEADS + hd + 1]
            i_row = gr[hd:hd + 1, :]
            b_row = loc["bc_row"][MLSTM_HEADS + hd:MLSTM_HEADS + hd + 1, :]
            m_prev = m_sc[hd][0:1, 0:1]
            state = st_sc[hd]
            qh = loc["q"][:, cols]
            kh = loc["k"][:, cols]
            vth = inp["vt"][cols, :]

            dmat = jnp.where(visible, b_row + c_col, -jnp.inf)
            m_inter = b_row + m_prev
            m_t = jnp.maximum(m_inter,
                              jnp.max(_tree_reduce(jnp.maximum, dmat), axis=0, keepdims=True))
            decay = jnp.exp(dmat - m_t)
            inter = jnp.exp(m_inter - m_t)
            scores = lax.dot_general(kh, qh, nt, preferred_element_type=F32) * decay
            sq = lax.dot_general(state.astype(BF16), qh, nt, preferred_element_type=F32)
            num = jnp.dot(vth, scores.astype(BF16), preferred_element_type=F32) + inter * sq[0:dh]
            den = (jnp.sum(_tree_reduce(jnp.add, scores), axis=0, keepdims=True)
                   + inter * sq[dh:dh + 1])
            hc = num / jnp.maximum(jnp.abs(den), jnp.exp(-m_t))

            mu = jnp.sum(_tree_reduce(jnp.add, hc), axis=0, keepdims=True) * (1.0 / dh)
            hz = hc - mu
            var = jnp.sum(_tree_reduce(jnp.add, hz * hz), axis=0, keepdims=True) * (1.0 / dh)
            hn = (hz * lax.rsqrt(var + LN_EPS)).T * hn_ref[:, cols]
            ymlstm_ref[0, rows, cols] = (inp["o_sig"][:, cols] * hn).astype(BF16)

            b_last = b_row[:, chunk - 1:chunk]
            g = b_last - b_row + i_row
            m_new = jnp.maximum(b_last + m_prev, jnp.max(g, axis=1, keepdims=True))
            wk = jnp.exp(g - m_new)
            carry = jnp.exp(b_last + m_prev - m_new)
            lhs = jnp.concatenate(
                [vth.astype(F32) * wk, jnp.broadcast_to(wk, (SUBLANES, chunk))], axis=0)
            st_sc[hd] = carry * state + jnp.dot(lhs.astype(BF16), kh, preferred_element_type=F32)
            m_sc[hd] = jnp.broadcast_to(m_new, (SUBLANES, LANES))

        def pool_step():
            t_glob = s * ts + r0 + lax.broadcasted_iota(jnp.int32, (chunk, 1), 0)
            for gi, win in enumerate(POOL_WINDOWS):
                cols = slice(gi * POOL_GROUP, (gi + 1) * POOL_GROUP)
                cur = pbuf[POOL_HALO + r0:POOL_HALO + r0 + chunk, cols]
                acc = cur
                for j in range(1, win):
                    acc = acc + pbuf[POOL_HALO + r0 - j:POOL_HALO + r0 - j + chunk, cols]
                cnt = jnp.minimum(t_glob + 1, win).astype(F32)
                pooled = acc / cnt - cur
                mixed = jnp.dot(pooled.astype(BF16), pw_ref[gi], preferred_element_type=F32)
                ypool_ref[0, rows, cols] = (mixed * ps_ref[:, cols]).astype(BF16)

        heads = [functools.partial(head_step, hd) for hd in range(MLSTM_HEADS)]
        return [prep_step] + heads + [pool_step]

    n_blocks = ts // chunk
    values = [dict() for _ in range(n_blocks)]
    for step in project_steps(0, values[0]):
        step()
    for c in range(n_blocks):
        ahead = project_steps(c + 1, values[c + 1]) if c + 1 < n_blocks else []
        steps = mix_steps(c, values[c])
        assert len(ahead) <= len(steps)
        for i, step in enumerate(steps):
            step()
            if i < len(ahead):
                ahead[i]()
    qkbuf[0:CONV_HALO, :] = qkbuf[ts:ts + CONV_HALO, :]
    pbuf[0:POOL_HALO, :] = pbuf[ts:ts + POOL_HALO, :]


def _even_mixer(x, mod, w_cat, w_vt, gate_b, pool_w, pool_scale, conv_qk, head_norm):
    bsz, seq, d = x.shape
    ts = SEQ_TILE
    tok = lambda width: pl.BlockSpec((1, ts, width), lambda b, s: (b, s, 0))
    return pl.pallas_call(
        functools.partial(_even_mixer_kernel, ts=ts, chunk=MLSTM_CHUNK),
        grid=(bsz, seq // ts),
        in_specs=[
            tok(d),
            pl.BlockSpec((1, 3, d), lambda b, s: (b, 0, 0)),
            _const_spec(w_cat.shape),
            _const_spec(w_vt.shape),
            _const_spec((1, GATE_PAD)),
            _const_spec(pool_w.shape),
            _const_spec((1, POOL_WIDTH)),
            _const_spec(conv_qk.shape),
            _const_spec((1, MLSTM_WIDTH)),
        ],
        out_specs=(tok(POOL_WIDTH), tok(MLSTM_WIDTH)),
        out_shape=(
            jax.ShapeDtypeStruct((bsz, seq, POOL_WIDTH), BF16),
            jax.ShapeDtypeStruct((bsz, seq, MLSTM_WIDTH), BF16),
        ),
        scratch_shapes=[
            pltpu.VMEM((POOL_HALO + ts, POOL_WIDTH), F32),
            pltpu.VMEM((CONV_HALO + ts, 2 * MLSTM_WIDTH), F32),
            pltpu.VMEM((MLSTM_HEADS, MLSTM_HEAD_DIM + SUBLANES, MLSTM_HEAD_DIM), F32),
            pltpu.VMEM((MLSTM_HEADS, SUBLANES, LANES), F32),
        ],
        compiler_params=_params(("parallel", "arbitrary"), 48),
        name="even_mixer",
    )(x, mod, w_cat, w_vt, gate_b, pool_w, pool_scale, conv_qk, head_norm)


def _mla_proj_kernel(x_ref, mod_ref, pos_ref, inv_ref, w_in_ref, qn_ref, kvn_ref, wq_ref, wk_ref, wvt_ref,
                     q_out, k_out, vt_out, *, ts):
    mod = mod_ref[0]
    shift, scale = mod[0:1], mod[1:2]
    h = (x_ref[0] * (1.0 + scale) + shift).astype(BF16)

    ang = inv_ref[...] * pos_ref[0]
    cos_t = jnp.cos(ang)
    sin_t = jnp.sin(ang)
    pairs = LANES // MLA_ROPE
    cos_all = jnp.concatenate([cos_t, cos_t] * pairs, axis=0).T
    sin_all = jnp.concatenate([-sin_t, sin_t] * pairs, axis=0).T

    def rms(a, g):
        return a * lax.rsqrt(jnp.mean(a * a, axis=-1, keepdims=True) + RMS_EPS) * g

    sm_scale = MLA_QK ** -0.5 * LOG2_E
    n_nope = MLA_HEADS * MLA_NOPE
    n_rope = MLA_HEADS * MLA_ROPE
    reps = n_rope // LANES
    nt = (((1,), (1,)), ((), ()))

    rb = ts // MLA_ROW_BLOCKS
    blocks = [slice(i * rb, (i + 1) * rb) for i in range(MLA_ROW_BLOCKS)]
    us = [jnp.dot(h[r], w_in_ref[...], preferred_element_type=F32) for r in blocks]
    for r, u in zip(blocks, us):
        cq = rms(u[:, :MLA_Q_LORA], qn_ref[...]).astype(BF16)
        ckv = rms(u[:, MLA_Q_LORA:MLA_Q_LORA + MLA_KV_LORA], kvn_ref[...]).astype(BF16)
        kr = u[:, MLA_Q_LORA + MLA_KV_LORA:]
        cos, sin = cos_all[r], sin_all[r]
        qa = jnp.dot(cq, wq_ref[...], preferred_element_type=F32) * sm_scale
        q_rope = (qa[:, n_nope:n_nope + n_rope] * jnp.concatenate([cos] * reps, axis=1)
                  + qa[:, n_nope + n_rope:] * jnp.concatenate([sin] * reps, axis=1))
        k_rope = (kr[:, :MLA_ROPE] * cos[:, :MLA_ROPE]
                  + kr[:, MLA_ROPE:] * sin[:, :MLA_ROPE]).astype(BF16)
        k_nope = jnp.dot(ckv, wk_ref[...], preferred_element_type=F32)
        vt = lax.dot_general(wvt_ref[...], ckv, nt, preferred_element_type=F32)
        for hd in range(MLA_HEADS):
            q_out[0, hd, r, 0:MLA_NOPE] = qa[:, hd * MLA_NOPE:(hd + 1) * MLA_NOPE].astype(BF16)
            q_out[0, hd, r, MLA_NOPE:MLA_QK] = q_rope[:, hd * MLA_ROPE:(hd + 1) * MLA_ROPE].astype(BF16)
            k_out[0, hd, r, 0:MLA_NOPE] = k_nope[:, hd * MLA_NOPE:(hd + 1) * MLA_NOPE].astype(BF16)
            k_out[0, hd, r, MLA_NOPE:MLA_QK] = k_rope
            vt_out[0, hd, 0, :, r] = vt[hd * MLA_V:(hd + 1) * MLA_V, :].astype(BF16)


def _mla_proj(x, mod, pos, inv, w_in, q_norm, kv_norm, w_uq, w_uk, w_uvt):
    bsz, seq, d = x.shape
    ts = SEQ_TILE
    assert ts == ATT_TILE
    head_spec = lambda w: pl.BlockSpec((1, MLA_HEADS, ts, w), lambda b, s: (b, 0, s, 0))
    return pl.pallas_call(
        functools.partial(_mla_proj_kernel, ts=ts),
        grid=(bsz, seq // ts),
        in_specs=[
            pl.BlockSpec((1, ts, d), lambda b, s: (b, s, 0)),
            pl.BlockSpec((1, 3, d), lambda b, s: (b, 0, 0)),
            pl.BlockSpec((1, 1, ts), lambda b, s: (b, 0, s)),
            _const_spec(inv.shape),
            _const_spec(w_in.shape),
            _const_spec((1, MLA_Q_LORA)), _const_spec((1, MLA_KV_LORA)),
            _const_spec(w_uq.shape), _const_spec(w_uk.shape), _const_spec(w_uvt.shape),
        ],
        out_specs=(
            head_spec(MLA_QK), head_spec(MLA_QK),
            pl.BlockSpec((1, MLA_HEADS, 1, MLA_V, ts), lambda b, s: (b, 0, s, 0, 0)),
        ),
        out_shape=(
            jax.ShapeDtypeStruct((bsz, MLA_HEADS, seq, MLA_QK), BF16),
            jax.ShapeDtypeStruct((bsz, MLA_HEADS, seq, MLA_QK), BF16),
            jax.ShapeDtypeStruct((bsz, MLA_HEADS, seq // ts, MLA_V, ts), BF16),
        ),
        compiler_params=_params(("parallel", "arbitrary"), 48),
        name="mla_proj",
    )(x, mod, pos, inv, w_in, q_norm, kv_norm, w_uq, w_uk, w_uvt)


def _attn_kernel(q_ref, k_ref, vt_ref, o_ref, m_sc, l_sc, acc_sc, *, tile, q_blocks):
    qi = pl.program_id(2)
    nt = (((1,), (1,)), ((), ()))

    def score_tile(chain):
        j, qb, _ = chain
        k = k_ref[0, 0, pl.ds(pl.multiple_of(j * tile, tile), tile), :]
        q = q_ref[0, 0, qb * tile:(qb + 1) * tile, :]
        return lax.dot_general(k, q, nt, preferred_element_type=F32)

    def for_each_chain(chains, consume):
        scores = []
        for idx, chain in enumerate(chains):
            while len(scores) < min(len(chains), idx + 1 + ATT_LOOKAHEAD):
                scores.append(score_tile(chains[len(scores)]))
            consume(chain, scores[idx])

    def col_max(a):
        return jnp.max(_tree_reduce(jnp.maximum, a), axis=0, keepdims=True)

    def col_sum(a):
        return jnp.sum(_tree_reduce(jnp.add, a), axis=0, keepdims=True)

    def pv(j, p):
        return jnp.dot(vt_ref[0, 0, j], p.astype(BF16), preferred_element_type=F32)

    def load_state(side):
        return [(m_sc[side, qb], l_sc[side, qb], acc_sc[side, qb]) for qb in range(q_blocks)]

    def store_state(side, state):
        for qb, (m, l, acc) in enumerate(state):
            m_sc[side, qb] = m
            l_sc[side, qb] = l
            acc_sc[side, qb] = acc

    def exact_update(state, chains):
        state = list(state)

        def consume(chain, sc):
            j, qb, masked = chain
            if masked:
                kc = lax.broadcasted_iota(jnp.int32, sc.shape, 0) // ATT_CHUNK
                qc = lax.broadcasted_iota(jnp.int32, sc.shape, 1) // ATT_CHUNK
                sc = jnp.where(qc >= kc, sc, -jnp.inf)
            m_prev, l_prev, acc_prev = state[qb]
            m_new = jnp.maximum(m_prev, col_max(sc))
            alpha = jnp.exp2(m_prev - m_new)
            p = jnp.exp2(sc - m_new)
            state[qb] = (m_new, alpha * l_prev + col_sum(p), alpha * acc_prev + pv(j, p))

        for_each_chain(chains, consume)
        return state

    def stale_update(state, chains):
        m0 = [st[0] for st in state]
        l = [st[1] for st in state]
        acc = [st[2] for st in state]
        top = [None] * q_blocks

        def consume(chain, sc):
            j, qb, _ = chain
            p = jnp.exp2(sc - m0[qb])
            tile_max = col_max(sc)
            l[qb] = l[qb] + col_sum(p)
            acc[qb] = acc[qb] + pv(j, p)
            top[qb] = tile_max if top[qb] is None else jnp.maximum(top[qb], tile_max)

        for_each_chain(chains, consume)
        new_state = []
        worst = None
        for qb in range(q_blocks):
            m_new = jnp.maximum(m0[qb], top[qb])
            alpha = jnp.exp2(m0[qb] - m_new)
            new_state.append((m_new, alpha * l[qb], alpha * acc[qb]))
            excess = top[qb] - m0[qb]
            worst = excess if worst is None else jnp.maximum(worst, excess)
        return new_state, worst

    j0 = qi * q_blocks
    diag = [(j0 + dj, qb, dj == qb) for dj in range(q_blocks) for qb in range(dj, q_blocks)]
    empty = [(jnp.full((1, tile), -jnp.inf, F32), jnp.zeros((1, tile), F32),
              jnp.zeros((acc_sc.shape[2], tile), F32)) for _ in range(q_blocks)]
    store_state(0, exact_update(empty, diag))

    def body(i, carry):
        src = i % 2
        chains = [(i * q_blocks + dj, qb, False) for dj in range(q_blocks) for qb in range(q_blocks)]
        state, worst = stale_update(load_state(src), chains)
        store_state(1 - src, state)

        @pl.when(jnp.max(worst) > ATT_STALE_MARGIN)
        def _():
            store_state(1 - src, exact_update(load_state(src), chains))

        return carry

    lax.fori_loop(0, qi, body, 0)
    last = qi % 2
    for qb in range(q_blocks):
        o_ref[0, qb * tile:(qb + 1) * tile, :] = (acc_sc[last, qb] / l_sc[last, qb]).T.astype(BF16)


def _attention(q, k, vt):
    bsz, heads, seq, dqk = q.shape
    n_kv, dv, tile = vt.shape[2:]
    q_blocks = ATT_Q_BLOCKS
    tq = tile * q_blocks
    return pl.pallas_call(
        functools.partial(_attn_kernel, tile=tile, q_blocks=q_blocks),
        grid=(bsz, heads, seq // tq),
        in_specs=[
            pl.BlockSpec((1, 1, tq, dqk), lambda b, h, i: (b, h, i, 0)),
            pl.BlockSpec((1, 1, seq, dqk), lambda b, h, i: (b, h, 0, 0)),
            pl.BlockSpec((1, 1, n_kv, dv, tile), lambda b, h, i: (b, h, 0, 0, 0)),
        ],
        out_specs=pl.BlockSpec((1, tq, dv), lambda b, h, i: (b, i, h)),
        out_shape=jax.ShapeDtypeStruct((bsz, seq, heads * dv), BF16),
        scratch_shapes=[
            pltpu.VMEM((2, q_blocks, 1, tile), F32),
            pltpu.VMEM((2, q_blocks, 1, tile), F32),
            pltpu.VMEM((2, q_blocks, dv, tile), F32),
        ],
        compiler_params=_params(("parallel", "parallel", "arbitrary"), 40),
        name="mla_attention",
    )(q, k, vt)


def _mix_ffn_kernel(*refs, widths, ts, tf):
    (x_ref, mmod_ref, mg_ref, mb_ref, wo_ref,
     fmod_ref, fg_ref, fb_ref, wup_ref, cw_ref, wdn_ref) = refs[:11]
    y_refs = refs[11:11 + len(widths)]
    out_ref, halo = refs[11 + len(widths):]
    s = pl.program_id(1)

    @pl.when(s == 0)
    def _():
        halo[...] = jnp.zeros(halo.shape, F32)

    fmod = fmod_ref[0]
    shift, scale, gate = fmod[0:1], fmod[1:2], fmod[2:3]
    mgate = mmod_ref[0][2:3]
    n_blocks = FFN_ROW_BLOCKS
    rb = ts // n_blocks
    sub = rb // FFN_SIDE_SPLIT
    n_f = D_FF // tf
    rowid = lax.broadcasted_iota(jnp.int32, (rb, tf), 0)
    sqrt_half = 0.5 ** 0.5

    def project(c):
        y = None
        r0 = 0
        for y_ref, width in zip(y_refs, widths):
            part = jnp.dot(y_ref[0, c * rb:(c + 1) * rb, :], wo_ref[r0:r0 + width, :],
                           preferred_element_type=F32)
            y = part if y is None else y + part
            r0 += width
        return y

    def norm_in(c, y, i):
        r = slice(c * rb + i * sub, c * rb + (i + 1) * sub)
        x1 = _layer_norm_rows(DEEPNORM_ALPHA * x_ref[0, r, :] + mgate * y[i * sub:(i + 1) * sub],
                              mg_ref[...], mb_ref[...])
        return x1, (x1 * (1.0 + scale) + shift).astype(BF16)

    def norm_out(c, x1, y, i):
        r = slice(i * sub, (i + 1) * sub)
        z = DEEPNORM_ALPHA * x1[r] + gate * y[r]
        out_ref[0, c * rb + i * sub:c * rb + (i + 1) * sub, :] = _layer_norm_rows(
            z, fg_ref[...], fb_ref[...])

    def up(hb, f):
        a = jnp.dot(hb, wup_ref[:, f * tf:(f + 1) * tf], preferred_element_type=F32)
        g = jnp.dot(hb, wup_ref[:, D_FF + f * tf:D_FF + (f + 1) * tf], preferred_element_type=F32)
        return a, g

    def down(c, f, a, g, prev):
        cols = slice(f * tf, (f + 1) * tf)
        p1 = prev[CONV_HALO - 1:CONV_HALO]
        p2 = prev[CONV_HALO - 2:CONV_HALO - 1]
        g1 = jnp.where(rowid == 0, p1, pltpu.roll(g, 1, axis=0))
        g2 = jnp.where(rowid == 0, p2, jnp.where(rowid == 1, p1, pltpu.roll(g, 2, axis=0)))
        cw = cw_ref[:, cols]
        gc = cw[0:1] * g2 + cw[1:2] * g1 + cw[2:3] * g
        act = a * (0.5 * gc * (1.0 + lax.erf(gc * sqrt_half)))
        return jnp.dot(act.astype(BF16), wdn_ref[cols, :], preferred_element_type=F32)

    ymix = [project(c) for c in range(n_blocks)]
    parts = [norm_in(0, ymix[0], i) for i in range(FFN_SIDE_SPLIT)]
    x1 = [jnp.concatenate([p[0] for p in parts], axis=0)] + [None] * (n_blocks - 1)
    hb = [jnp.concatenate([p[1] for p in parts], axis=0)] + [None] * (n_blocks - 1)
    yffn = [None] * n_blocks
    tails = [None] * n_f
    nxt = up(hb[0], 0)
    for c in range(n_blocks):
        side = []
        if c + 1 < n_blocks:
            side += [("in", i) for i in range(FFN_SIDE_SPLIT)]
        if c > 0:
            side += [("out", i) for i in range(FFN_SIDE_SPLIT)]
        in_parts = []
        y = None
        for f in range(n_f):
            a, g = nxt
            if f + 1 < n_f:
                nxt = up(hb[c], f + 1)
            elif c + 1 < n_blocks:
                nxt = up(hb[c + 1], 0)
            cols = slice(f * tf, (f + 1) * tf)
            prev = halo[:, cols] if c == 0 else tails[f]
            tails[f] = g[rb - CONV_HALO:rb, :]
            if c == n_blocks - 1:
                halo[:, cols] = tails[f]
            part = down(c, f, a, g, prev)
            y = part if y is None else y + part
            if f < len(side):
                kind, i = side[f]
                if kind == "in":
                    in_parts.append(norm_in(c + 1, ymix[c + 1], i))
                    if i == FFN_SIDE_SPLIT - 1:
                        x1[c + 1] = jnp.concatenate([p[0] for p in in_parts], axis=0)
                        hb[c + 1] = jnp.concatenate([p[1] for p in in_parts], axis=0)
                else:
                    norm_out(c - 1, x1[c - 1], yffn[c - 1], i)
        assert len(side) <= n_f
        yffn[c] = y
    for i in range(FFN_SIDE_SPLIT):
        norm_out(n_blocks - 1, x1[n_blocks - 1], yffn[n_blocks - 1], i)


def _mix_ffn(x, ys, mix_mod, mix_g, mix_b, w_out, ffn_mod, ffn_g, ffn_b, w_up, conv_w, w_down):
    bsz, seq, d = x.shape
    ts, tf = FFN_SEQ_TILE, FFN_TILE
    assert D_FF % tf == 0 and seq % ts == 0
    widths = tuple(y.shape[-1] for y in ys)
    tok = lambda w: pl.BlockSpec((1, ts, w), lambda b, s: (b, s, 0))
    mod_spec = pl.BlockSpec((1, 3, d), lambda b, s: (b, 0, 0))
    return pl.pallas_call(
        functools.partial(_mix_ffn_kernel, widths=widths, ts=ts, tf=tf),
        grid=(bsz, seq // ts),
        in_specs=[
            tok(d),
            mod_spec, _const_spec((1, d)), _const_spec((1, d)), _const_spec(w_out.shape),
            mod_spec, _const_spec((1, d)), _const_spec((1, d)),
            _const_spec(w_up.shape), _const_spec(conv_w.shape), _const_spec(w_down.shape),
        ] + [tok(w) for w in widths],
        out_specs=tok(d),
        out_shape=jax.ShapeDtypeStruct((bsz, seq, d), F32),
        scratch_shapes=[pltpu.VMEM((CONV_HALO, D_FF), F32)],
        compiler_params=_params(("parallel", "arbitrary"), 56),
        name="mix_ffn",
    )(x, mix_mod, mix_g, mix_b, w_out, ffn_mod, ffn_g, ffn_b, w_up, conv_w, w_down, *ys)


def _swap_halves(w):
    half = w.shape[-1] // 2
    return jnp.concatenate([w[..., half:], w[..., :half]], axis=-1)


def kernel(x, c, positions, e_ada_w, e_ada_b, e_w_in, e_pool_w, e_pool_scale, e_conv_qk, e_gate_b, e_head_norm, e_w_out, e_ln_g, e_ln_b, o_ada_w, o_ada_b, o_w_in, o_q_norm, o_kv_norm, o_w_uq, o_w_ukv, o_w_out, o_ln_g, o_ln_b, f_ada_w, f_ada_b, f_w_up, f_conv, f_w_down, f_ln_g, f_ln_b):
    bsz, seq, d = x.shape
    row = lambda a: a.reshape(1, -1)

    half = MLA_ROPE // 2
    inv = ROPE_THETA ** (-jnp.arange(half, dtype=F32) / half)
    inv_col = inv.reshape(half, 1)
    pos = positions.astype(F32).reshape(bsz, 1, seq)

    for layer in range(DEPTH):
        j = layer // 2
        if layer % 2 == 0:
            mod = _ada_modulation(c, e_ada_w, e_ada_b, j)
            w_in = e_w_in[j]
            v0 = POOL_WIDTH + 2 * MLSTM_WIDTH
            gate_pad = jnp.zeros((d, GATE_PAD - 2 * MLSTM_HEADS), F32)
            w_cat = jnp.concatenate(
                [w_in[:, :v0], w_in[:, v0 + MLSTM_WIDTH:], gate_pad], axis=1).astype(BF16)
            w_vt = w_in[:, v0:v0 + MLSTM_WIDTH].T.astype(BF16)
            gate_b = jnp.pad(e_gate_b[j], (0, GATE_PAD - 2 * MLSTM_HEADS)).reshape(1, GATE_PAD)
            y_pool, y_mlstm = _even_mixer(
                x, mod, w_cat, w_vt, gate_b, e_pool_w[j].astype(BF16), row(e_pool_scale[j]),
                e_conv_qk[j], row(e_head_norm[j]))
            ys = (y_pool, y_mlstm)
            mix = (mod, row(e_ln_g[j]), row(e_ln_b[j]), e_w_out[j].astype(BF16))
        else:
            mod = _ada_modulation(c, o_ada_w, o_ada_b, j)
            w_in = o_w_in[j]
            k_r = w_in[:, MLA_Q_LORA + MLA_KV_LORA:]
            w_in_cat = jnp.concatenate([w_in, _swap_halves(k_r)], axis=1).astype(BF16)
            wq = o_w_uq[j].reshape(MLA_Q_LORA, MLA_HEADS, MLA_QK)
            wq_rope = wq[:, :, MLA_NOPE:]
            wq_cat = jnp.concatenate([
                wq[:, :, :MLA_NOPE].reshape(MLA_Q_LORA, -1),
                wq_rope.reshape(MLA_Q_LORA, -1),
                _swap_halves(wq_rope).reshape(MLA_Q_LORA, -1)], axis=1).astype(BF16)
            wkv = o_w_ukv[j].reshape(MLA_KV_LORA, MLA_HEADS, MLA_NOPE + MLA_V)
            wk = wkv[:, :, :MLA_NOPE].reshape(MLA_KV_LORA, -1).astype(BF16)
            wvt = wkv[:, :, MLA_NOPE:].reshape(MLA_KV_LORA, -1).T.astype(BF16)
            qh, kh, vth = _mla_proj(x, mod, pos, inv_col, w_in_cat, row(o_q_norm[j]),
                                    row(o_kv_norm[j]), wq_cat, wk, wvt)
            ys = (_attention(qh, kh, vth),)
            mix = (mod, row(o_ln_g[j]), row(o_ln_b[j]), o_w_out[j].astype(BF16))
        ffn_mod = _ada_modulation(c, f_ada_w, f_ada_b, layer)
        x = _mix_ffn(x, ys, *mix, ffn_mod, row(f_ln_g[layer]), row(f_ln_b[layer]),
                     f_w_up[layer].astype(BF16), f_conv[layer], f_w_down[layer].astype(BF16))
    return x
```

```python
import functools

import jax
import jax.numpy as jnp
from jax import lax
from jax.experimental import pallas as pl
from jax.experimental.pallas import tpu as pltpu

F32 = jnp.float32
BF16 = jnp.bfloat16
HIGHEST = lax.Precision.HIGHEST

D_MODEL = 1024
DEPTH = 2
POOL_WINDOWS = (2, 4, 8, 16)
POOL_GROUP = 128
POOL_WIDTH = 512
POOL_HALO = 16
MLSTM_HEADS = 4
MLSTM_HEAD_DIM = 128
MLSTM_WIDTH = 512
MLSTM_CONV = 4
CONV_HALO = 8
GATE_PAD = 128
MLA_HEADS = 8
MLA_NOPE = 128
MLA_ROPE = 64
MLA_V = 128
MLA_QK = MLA_NOPE + MLA_ROPE
MLA_Q_LORA = 512
MLA_KV_LORA = 256
ROPE_THETA = 10000.0
D_FF = 2816
FFN_CONV = 3
DEEPNORM_ALPHA = (2 * DEPTH) ** 0.25
LN_EPS = 1e-5
RMS_EPS = 1e-6
LOG2_E = 1.4426950408889634

VMEM_CAP_BYTES = 64 * 1024 * 1024
LANES = 128
SUBLANES = 8
REDUCE_CHAINS = 4

SEQ_TILE = 512
EVEN_SEQ_TILE = 1024
MLSTM_CHUNK = 256
ATT_TILE = 512
MLA_ROW_BLOCKS = 2
ATT_Q_BLOCKS = 8
ATT_LOOKAHEAD = 3
ATT_CHUNK = 64
ATT_STALE_MARGIN = 64.0
FFN_TILE = 256
FFN_SEQ_TILE = 1024
FFN_ROW_BLOCKS = 4
FFN_SIDE_SPLIT = 4
ADA_TILE = 512


def _params(semantics, vmem_mib):
    limit = vmem_mib * 1024 * 1024
    assert limit <= VMEM_CAP_BYTES
    return pltpu.CompilerParams(dimension_semantics=semantics, vmem_limit_bytes=limit)


def _const_spec(shape):
    nd = len(shape)
    return pl.BlockSpec(shape, lambda *_: (0,) * nd, pipeline_mode=pl.Buffered(1))


def _layer_norm_rows(z, g, b):
    mu = jnp.mean(z, axis=-1, keepdims=True)
    zc = z - mu
    var = jnp.mean(zc * zc, axis=-1, keepdims=True)
    return zc * lax.rsqrt(var + LN_EPS) * g + b


def _sigmoid(x):
    return 1.0 / (1.0 + jnp.exp(-x))


def _tree_reduce(op, a):
    slab = SUBLANES * REDUCE_CHAINS
    if a.shape[0] > slab and a.shape[0] % slab == 0:
        acc = a[:slab]
        for r0 in range(slab, a.shape[0], slab):
            acc = op(acc, a[r0:r0 + slab])
        a = acc
    while a.shape[0] > SUBLANES:
        half = a.shape[0] // 2
        a = op(a[:half], a[half:])
    return a


def _split_bf16(a):
    hi = a.astype(BF16)
    r1 = a - hi.astype(F32)
    mid = r1.astype(BF16)
    lo = (r1 - mid.astype(F32)).astype(BF16)
    return hi, mid, lo


def _ada_kernel(c_ref, w_ref, b_ref, o_ref):
    c = c_ref[...]
    cs = c * _sigmoid(c)
    o_ref[...] = jnp.dot(cs, w_ref[...], precision=HIGHEST, preferred_element_type=F32) + b_ref[...]


def _ada_modulation(c, w, b, layer):
    bsz, d = c.shape
    n_out = w.shape[-1]
    b3 = b.reshape(b.shape[0], 1, n_out)
    out = pl.pallas_call(
        _ada_kernel,
        grid=(n_out // ADA_TILE,),
        in_specs=[
            pl.BlockSpec((bsz, d), lambda n: (0, 0)),
            pl.BlockSpec((None, d, ADA_TILE), lambda n: (layer, 0, n)),
            pl.BlockSpec((None, 1, ADA_TILE), lambda n: (layer, 0, n)),
        ],
        out_specs=pl.BlockSpec((bsz, ADA_TILE), lambda n: (0, n)),
        out_shape=jax.ShapeDtypeStruct((bsz, n_out), F32),
        compiler_params=_params(("arbitrary",), 16),
        name="ada_modulation",
    )(c, w, b3)
    return out.reshape(bsz, 3, d)


def _even_mixer_kernel(x_ref, mod_ref, w_ref, wvt_ref, gb_ref, pw_ref, ps_ref, cw_ref, hn_ref,
                       ypool_ref, ymlstm_ref, pbuf, qkbuf, st_sc, m_sc, *, ts, chunk):
    s = pl.program_id(1)

    @pl.when(s == 0)
    def _():
        pbuf[0:POOL_HALO, :] = jnp.zeros((POOL_HALO, POOL_WIDTH), F32)
        qkbuf[0:CONV_HALO, :] = jnp.zeros((CONV_HALO, 2 * MLSTM_WIDTH), F32)
        st_sc[...] = jnp.zeros(st_sc.shape, F32)
        m_sc[...] = jnp.zeros(m_sc.shape, F32)

    mod = mod_ref[0]
    shift, scale = mod[0:1], mod[1:2]
    h = (x_ref[0] * (1.0 + scale) + shift).astype(BF16)

    nt = (((1,), (1,)), ((), ()))
    c_qk = POOL_WIDTH
    c_o = c_qk + 2 * MLSTM_WIDTH
    c_g = c_o + MLSTM_WIDTH
    dh = MLSTM_HEAD_DIM
    key = lax.broadcasted_iota(jnp.int32, (chunk, chunk), 0)
    qry = lax.broadcasted_iota(jnp.int32, (chunk, chunk), 1)
    visible = key <= qry
    lower = (qry <= key).astype(BF16)
    upper = visible.astype(BF16)
    cw = cw_ref[...]

    def project_steps(c, out):
        r0 = c * chunk
        hc = h[r0:r0 + chunk]

        def qk_step():
            qkbuf[CONV_HALO + r0:CONV_HALO + r0 + chunk, :] = jnp.dot(
                hc, w_ref[:, c_qk:c_o], preferred_element_type=F32)

        def gate_step():
            out["gates"] = (jnp.dot(hc, w_ref[:, c_g:c_g + GATE_PAD], preferred_element_type=F32)
                            + gb_ref[...])

        def v_step():
            out["vt"] = lax.dot_general(wvt_ref[...], hc, nt,
                                        preferred_element_type=F32).astype(BF16)

        def o_step():
            out["o_sig"] = _sigmoid(jnp.dot(hc, w_ref[:, c_o:c_g], preferred_element_type=F32))

        def pool_step():
            pbuf[POOL_HALO + r0:POOL_HALO + r0 + chunk, :] = jnp.dot(
                hc, w_ref[:, 0:c_qk], preferred_element_type=F32)

        return [qk_step, gate_step, v_step, o_step, pool_step]

    def mix_steps(c, inp):
        r0 = c * chunk
        rows = slice(r0, r0 + chunk)
        loc = {}

        def prep_step():
            gates = inp["gates"]
            lane = lax.broadcasted_iota(jnp.int32, gates.shape, 1)
            log_f = jnp.minimum(gates, 0.0) - jnp.log1p(jnp.exp(-jnp.abs(gates)))
            gc = jnp.where(lane >= MLSTM_HEADS, log_f, gates)
            gr = gc.T[0:2 * MLSTM_HEADS, :]
            acc = qkbuf[CONV_HALO + r0:CONV_HALO + r0 + chunk, :] * cw[MLSTM_CONV - 1:MLSTM_CONV]
            for j in range(MLSTM_CONV - 1):
                off = CONV_HALO - (MLSTM_CONV - 1) + j + r0
                acc = acc + qkbuf[off:off + chunk, :] * cw[j:j + 1]
            qk = acc * _sigmoid(acc)
            loc["q"] = qk[:, :MLSTM_WIDTH].astype(BF16)
            loc["k"] = (qk[:, MLSTM_WIDTH:] * (dh ** -0.5)).astype(BF16)
            loc["bc_col"] = sum(jnp.dot(lower, part, preferred_element_type=F32)
                                for part in _split_bf16(gc))
            loc["bc_row"] = sum(jnp.dot(part, upper, preferred_element_type=F32)
                                for part in _split_bf16(gr))
            loc["gc"], loc["gr"] = gc, gr

        def head_step(hd):
            gc, gr = loc["gc"], loc["gr"]
            cols = slice(hd * dh, (hd + 1) * dh)
            c_col = gc[:, hd:hd + 1] - loc["bc_col"][:, MLSTM_HEADS + hd:MLSTM_HEADS + hd + 1]
            i_row = gr[hd:hd + 1, :]
            b_row = loc["bc_row"][MLSTM_HEADS + hd:MLSTM_HEADS + hd + 1, :]
            m_prev = m_sc[hd][0:1, 0:1]
            state = st_sc[hd]
            qh = loc["q"][:, cols]
            kh = loc["k"][:, cols]
            vth = inp["vt"][cols, :]

            dmat = jnp.where(visible, b_row + c_col, -jnp.inf)
            m_inter = b_row + m_prev
            m_t = jnp.maximum(m_inter,
                              jnp.max(_tree_reduce(jnp.maximum, dmat), axis=0, keepdims=True))
            decay = jnp.exp(dmat - m_t)
            inter = jnp.exp(m_inter - m_t)
            scores = lax.dot_general(kh, qh, nt, preferred_element_type=F32) * decay
            sq = lax.dot_general(state.astype(BF16), qh, nt, preferred_element_type=F32)
            num = jnp.dot(vth, scores.astype(BF16), preferred_element_type=F32) + inter * sq[0:dh]
            den = (jnp.sum(_tree_reduce(jnp.add, scores), axis=0, keepdims=True)
                   + inter * sq[dh:dh + 1])
            hc = num / jnp.maximum(jnp.abs(den), jnp.exp(-m_t))

            mu = jnp.sum(_tree_reduce(jnp.add, hc), axis=0, keepdims=True) * (1.0 / dh)
            hz = hc - mu
            var = jnp.sum(_tree_reduce(jnp.add, hz * hz), axis=0, keepdims=True) * (1.0 / dh)
            hn = (hz * lax.rsqrt(var + LN_EPS)).T * hn_ref[:, cols]
            ymlstm_ref[0, rows, cols] = (inp["o_sig"][:, cols] * hn).astype(BF16)

            b_last = b_row[:, chunk - 1:chunk]
            g = b_last - b_row + i_row
            m_new = jnp.maximum(b_last + m_prev, jnp.max(g, axis=1, keepdims=True))
            wk = jnp.exp(g - m_new)
            carry = jnp.exp(b_last + m_prev - m_new)
            lhs = jnp.concatenate(
                [vth.astype(F32) * wk, jnp.broadcast_to(wk, (SUBLANES, chunk))], axis=0)
            st_sc[hd] = carry * state + jnp.dot(lhs.astype(BF16), kh, preferred_element_type=F32)
            m_sc[hd] = jnp.broadcast_to(m_new, (SUBLANES, LANES))

        def pool_step():
            t_glob = s * ts + r0 + lax.broadcasted_iota(jnp.int32, (chunk, 1), 0)
            for gi, win in enumerate(POOL_WINDOWS):
                cols = slice(gi * POOL_GROUP, (gi + 1) * POOL_GROUP)
                cur = pbuf[POOL_HALO + r0:POOL_HALO + r0 + chunk, cols]
                acc = cur
                for j in range(1, win):
                    acc = acc + pbuf[POOL_HALO + r0 - j:POOL_HALO + r0 - j + chunk, cols]
                cnt = jnp.minimum(t_glob + 1, win).astype(F32)
                pooled = acc / cnt - cur
                mixed = jnp.dot(pooled.astype(BF16), pw_ref[gi], preferred_element_type=F32)
                ypool_ref[0, rows, cols] = (mixed * ps_ref[:, cols]).astype(BF16)

        heads = [functools.partial(head_step, hd) for hd in range(MLSTM_HEADS)]
        return [prep_step] + heads + [pool_step]

    n_blocks = ts // chunk
    values = [dict() for _ in range(n_blocks)]
    for step in project_steps(0, values[0]):
        step()
    for c in range(n_blocks):
        ahead = project_steps(c + 1, values[c + 1]) if c + 1 < n_blocks else []
        steps = mix_steps(c, values[c])
        assert len(ahead) <= len(steps)
        for i, step in enumerate(steps):
            step()
            if i < len(ahead):
                ahead[i]()
    qkbuf[0:CONV_HALO, :] = qkbuf[ts:ts + CONV_HALO, :]
    pbuf[0:POOL_HALO, :] = pbuf[ts:ts + POOL_HALO, :]


def _even_mixer(x, mod, w_cat, w_vt, gate_b, pool_w, pool_scale, conv_qk, head_norm):
    bsz, seq, d = x.shape
    ts = EVEN_SEQ_TILE
    assert seq % ts == 0 and ts % MLSTM_CHUNK == 0
    tok = lambda width: pl.BlockSpec((1, ts, width), lambda b, s: (b, s, 0))
    return pl.pallas_call(
        functools.partial(_even_mixer_kernel, ts=ts, chunk=MLSTM_CHUNK),
        grid=(bsz, seq // ts),
        in_specs=[
            tok(d),
            pl.BlockSpec((1, 3, d), lambda b, s: (b, 0, 0)),
            _const_spec(w_cat.shape),
            _const_spec(w_vt.shape),
            _const_spec((1, GATE_PAD)),
            _const_spec(pool_w.shape),
            _const_spec((1, POOL_WIDTH)),
            _const_spec(conv_qk.shape),
            _const_spec((1, MLSTM_WIDTH)),
        ],
        out_specs=(tok(POOL_WIDTH), tok(MLSTM_WIDTH)),
        out_shape=(
            jax.ShapeDtypeStruct((bsz, seq, POOL_WIDTH), BF16),
            jax.ShapeDtypeStruct((bsz, seq, MLSTM_WIDTH), BF16),
        ),
        scratch_shapes=[
            pltpu.VMEM((POOL_HALO + ts, POOL_WIDTH), F32),
            pltpu.VMEM((CONV_HALO + ts, 2 * MLSTM_WIDTH), F32),
            pltpu.VMEM((MLSTM_HEADS, MLSTM_HEAD_DIM + SUBLANES, MLSTM_HEAD_DIM), F32),
            pltpu.VMEM((MLSTM_HEADS, SUBLANES, LANES), F32),
        ],
        compiler_params=_params(("parallel", "arbitrary"), 48),
        name="even_mixer",
    )(x, mod, w_cat, w_vt, gate_b, pool_w, pool_scale, conv_qk, head_norm)


def _mla_proj_kernel(x_ref, mod_ref, pos_ref, inv_ref, w_in_ref, qn_ref, kvn_ref, wq_ref, wk_ref, wvt_ref,
                     q_out, k_out, vt_out, *, ts):
    mod = mod_ref[0]
    shift, scale = mod[0:1], mod[1:2]
    h = (x_ref[0] * (1.0 + scale) + shift).astype(BF16)

    ang = inv_ref[...] * pos_ref[0]
    cos_t = jnp.cos(ang)
    sin_t = jnp.sin(ang)
    pairs = LANES // MLA_ROPE
    cos_all = jnp.concatenate([cos_t, cos_t] * pairs, axis=0).T
    sin_all = jnp.concatenate([-sin_t, sin_t] * pairs, axis=0).T

    def rms(a, g):
        return a * lax.rsqrt(jnp.mean(a * a, axis=-1, keepdims=True) + RMS_EPS) * g

    sm_scale = MLA_QK ** -0.5 * LOG2_E
    n_nope = MLA_HEADS * MLA_NOPE
    n_rope = MLA_HEADS * MLA_ROPE
    reps = n_rope // LANES
    nt = (((1,), (1,)), ((), ()))

    rb = ts // MLA_ROW_BLOCKS
    blocks = [slice(i * rb, (i + 1) * rb) for i in range(MLA_ROW_BLOCKS)]
    us = [jnp.dot(h[r], w_in_ref[...], preferred_element_type=F32) for r in blocks]
    for r, u in zip(blocks, us):
        cq = rms(u[:, :MLA_Q_LORA], qn_ref[...]).astype(BF16)
        ckv = rms(u[:, MLA_Q_LORA:MLA_Q_LORA + MLA_KV_LORA], kvn_ref[...]).astype(BF16)
        kr = u[:, MLA_Q_LORA + MLA_KV_LORA:]
        cos, sin = cos_all[r], sin_all[r]
        qa = jnp.dot(cq, wq_ref[...], preferred_element_type=F32) * sm_scale
        q_rope = (qa[:, n_nope:n_nope + n_rope] * jnp.concatenate([cos] * reps, axis=1)
                  + qa[:, n_nope + n_rope:] * jnp.concatenate([sin] * reps, axis=1))
        k_rope = (kr[:, :MLA_ROPE] * cos[:, :MLA_ROPE]
                  + kr[:, MLA_ROPE:] * sin[:, :MLA_ROPE]).astype(BF16)
        k_nope = jnp.dot(ckv, wk_ref[...], preferred_element_type=F32)
        vt = lax.dot_general(wvt_ref[...], ckv, nt, preferred_element_type=F32)
        for hd in range(MLA_HEADS):
            q_out[0, hd, r, 0:MLA_NOPE] = qa[:, hd * MLA_NOPE:(hd + 1) * MLA_NOPE].astype(BF16)
            q_out[0, hd, r, MLA_NOPE:MLA_QK] = q_rope[:, hd * MLA_ROPE:(hd + 1) * MLA_ROPE].astype(BF16)
            k_out[0, hd, r, 0:MLA_NOPE] = k_nope[:, hd * MLA_NOPE:(hd + 1) * MLA_NOPE].astype(BF16)
            k_out[0, hd, r, MLA_NOPE:MLA_QK] = k_rope
            vt_out[0, hd, 0, :, r] = vt[hd * MLA_V:(hd + 1) * MLA_V, :].astype(BF16)


def _mla_proj(x, mod, pos, inv, w_in, q_norm, kv_norm, w_uq, w_uk, w_uvt):
    bsz, seq, d = x.shape
    ts = SEQ_TILE
    assert ts == ATT_TILE
    head_spec = lambda w: pl.BlockSpec((1, MLA_HEADS, ts, w), lambda b, s: (b, 0, s, 0))
    return pl.pallas_call(
        functools.partial(_mla_proj_kernel, ts=ts),
        grid=(bsz, seq // ts),
        in_specs=[
            pl.BlockSpec((1, ts, d), lambda b, s: (b, s, 0)),
            pl.BlockSpec((1, 3, d), lambda b, s: (b, 0, 0)),
            pl.BlockSpec((1, 1, ts), lambda b, s: (b, 0, s)),
            _const_spec(inv.shape),
            _const_spec(w_in.shape),
            _const_spec((1, MLA_Q_LORA)), _const_spec((1, MLA_KV_LORA)),
            _const_spec(w_uq.shape), _const_spec(w_uk.shape), _const_spec(w_uvt.shape),
        ],
        out_specs=(
            head_spec(MLA_QK), head_spec(MLA_QK),
            pl.BlockSpec((1, MLA_HEADS, 1, MLA_V, ts), lambda b, s: (b, 0, s, 0, 0)),
        ),
        out_shape=(
            jax.ShapeDtypeStruct((bsz, MLA_HEADS, seq, MLA_QK), BF16),
            jax.ShapeDtypeStruct((bsz, MLA_HEADS, seq, MLA_QK), BF16),
            jax.ShapeDtypeStruct((bsz, MLA_HEADS, seq // ts, MLA_V, ts), BF16),
        ),
        compiler_params=_params(("parallel", "arbitrary"), 48),
        name="mla_proj",
    )(x, mod, pos, inv, w_in, q_norm, kv_norm, w_uq, w_uk, w_uvt)


def _attn_kernel(q_ref, k_ref, vt_ref, o_ref, m_sc, l_sc, acc_sc, *, tile, q_blocks):
    qi = pl.program_id(2)
    nt = (((1,), (1,)), ((), ()))

    def score_tile(chain):
        j, qb, _ = chain
        k = k_ref[0, 0, pl.ds(pl.multiple_of(j * tile, tile), tile), :]
        q = q_ref[0, 0, qb * tile:(qb + 1) * tile, :]
        return lax.dot_general(k, q, nt, preferred_element_type=F32)

    def for_each_chain(chains, consume):
        scores = []
        for idx, chain in enumerate(chains):
            while len(scores) < min(len(chains), idx + 1 + ATT_LOOKAHEAD):
                scores.append(score_tile(chains[len(scores)]))
            consume(chain, scores[idx])

    def col_max(a):
        return jnp.max(_tree_reduce(jnp.maximum, a), axis=0, keepdims=True)

    def col_sum(a):
        return jnp.sum(_tree_reduce(jnp.add, a), axis=0, keepdims=True)

    def pv(j, p):
        return jnp.dot(vt_ref[0, 0, j], p.astype(BF16), preferred_element_type=F32)

    def load_state(side):
        return [(m_sc[side, qb], l_sc[side, qb], acc_sc[side, qb]) for qb in range(q_blocks)]

    def store_state(side, state):
        for qb, (m, l, acc) in enumerate(state):
            m_sc[side, qb] = m
            l_sc[side, qb] = l
            acc_sc[side, qb] = acc

    def exact_update(state, chains):
        state = list(state)

        def consume(chain, sc):
            j, qb, masked = chain
            if masked:
                kc = lax.broadcasted_iota(jnp.int32, sc.shape, 0) // ATT_CHUNK
                qc = lax.broadcasted_iota(jnp.int32, sc.shape, 1) // ATT_CHUNK
                sc = jnp.where(qc >= kc, sc, -jnp.inf)
            m_prev, l_prev, acc_prev = state[qb]
            m_new = jnp.maximum(m_prev, col_max(sc))
            alpha = jnp.exp2(m_prev - m_new)
            p = jnp.exp2(sc - m_new)
            state[qb] = (m_new, alpha * l_prev + col_sum(p), alpha * acc_prev + pv(j, p))

        for_each_chain(chains, consume)
        return state

    def stale_update(state, chains):
        m0 = [st[0] for st in state]
        l = [st[1] for st in state]
        acc = [st[2] for st in state]
        top = [None] * q_blocks

        def consume(chain, sc):
            j, qb, _ = chain
            p = jnp.exp2(sc - m0[qb])
            tile_max = col_max(sc)
            l[qb] = l[qb] + col_sum(p)
            acc[qb] = acc[qb] + pv(j, p)
            top[qb] = tile_max if top[qb] is None else jnp.maximum(top[qb], tile_max)

        for_each_chain(chains, consume)
        new_state = []
        worst = None
        for qb in range(q_blocks):
            if top[qb] is None:
                new_state.append(state[qb])
                continue
            m_new = jnp.maximum(m0[qb], top[qb])
            alpha = jnp.exp2(m0[qb] - m_new)
            new_state.append((m_new, alpha * l[qb], alpha * acc[qb]))
            excess = top[qb] - m0[qb]
            worst = excess if worst is None else jnp.maximum(worst, excess)
        return new_state, worst

    j0 = qi * q_blocks
    empty = [(jnp.full((1, tile), -jnp.inf, F32), jnp.zeros((1, tile), F32),
              jnp.zeros((acc_sc.shape[2], tile), F32)) for _ in range(q_blocks)]
    store_state(0, exact_update(empty, [(j0 + qb, qb, True) for qb in range(q_blocks)]))

    def stale_group(src, chains):
        state, worst = stale_update(load_state(src), chains)
        store_state(1 - src, state)

        @pl.when(jnp.max(worst) > ATT_STALE_MARGIN)
        def _():
            store_state(1 - src, exact_update(load_state(src), chains))

    below = [(j0 + dj, qb, False) for dj in range(q_blocks) for qb in range(dj + 1, q_blocks)]
    if below:
        stale_group(0, below)
    first = 1 if below else 0

    def body(i, carry):
        stale_group((first + i) % 2,
                    [(i * q_blocks + dj, qb, False) for dj in range(q_blocks) for qb in range(q_blocks)])
        return carry

    lax.fori_loop(0, qi, body, 0)
    last = (first + qi) % 2
    for qb in range(q_blocks):
        o_ref[0, qb * tile:(qb + 1) * tile, :] = (acc_sc[last, qb] / l_sc[last, qb]).T.astype(BF16)


def _attention(q, k, vt):
    bsz, heads, seq, dqk = q.shape
    n_kv, dv, tile = vt.shape[2:]
    q_blocks = ATT_Q_BLOCKS
    tq = tile * q_blocks
    return pl.pallas_call(
        functools.partial(_attn_kernel, tile=tile, q_blocks=q_blocks),
        grid=(bsz, heads, seq // tq),
        in_specs=[
            pl.BlockSpec((1, 1, tq, dqk), lambda b, h, i: (b, h, i, 0)),
            pl.BlockSpec((1, 1, seq, dqk), lambda b, h, i: (b, h, 0, 0)),
            pl.BlockSpec((1, 1, n_kv, dv, tile), lambda b, h, i: (b, h, 0, 0, 0)),
        ],
        out_specs=pl.BlockSpec((1, tq, dv), lambda b, h, i: (b, i, h)),
        out_shape=jax.ShapeDtypeStruct((bsz, seq, heads * dv), BF16),
        scratch_shapes=[
            pltpu.VMEM((2, q_blocks, 1, tile), F32),
            pltpu.VMEM((2, q_blocks, 1, tile), F32),
            pltpu.VMEM((2, q_blocks, dv, tile), F32),
        ],
        compiler_params=_params(("parallel", "parallel", "arbitrary"), 48),
        name="mla_attention",
    )(q, k, vt)


def _mix_ffn_kernel(*refs, widths, ts, tf):
    (x_ref, mmod_ref, mg_ref, mb_ref, wo_ref,
     fmod_ref, fg_ref, fb_ref, wup_ref, cw_ref, wdn_ref) = refs[:11]
    y_refs = refs[11:11 + len(widths)]
    out_ref, halo = refs[11 + len(widths):]
    s = pl.program_id(1)

    @pl.when(s == 0)
    def _():
        halo[...] = jnp.zeros(halo.shape, F32)

    fmod = fmod_ref[0]
    shift, scale, gate = fmod[0:1], fmod[1:2], fmod[2:3]
    mgate = mmod_ref[0][2:3]
    n_blocks = FFN_ROW_BLOCKS
    rb = ts // n_blocks
    sub = rb // FFN_SIDE_SPLIT
    n_f = D_FF // tf
    rowid = lax.broadcasted_iota(jnp.int32, (rb, tf), 0)
    sqrt_half = 0.5 ** 0.5

    def project(c):
        y = None
        r0 = 0
        for y_ref, width in zip(y_refs, widths):
            part = jnp.dot(y_ref[0, c * rb:(c + 1) * rb, :], wo_ref[r0:r0 + width, :],
                           preferred_element_type=F32)
            y = part if y is None else y + part
            r0 += width
        return y

    def norm_in(c, y, i):
        r = slice(c * rb + i * sub, c * rb + (i + 1) * sub)
        x1 = _layer_norm_rows(DEEPNORM_ALPHA * x_ref[0, r, :] + mgate * y[i * sub:(i + 1) * sub],
                              mg_ref[...], mb_ref[...])
        return x1, (x1 * (1.0 + scale) + shift).astype(BF16)

    def norm_out(c, x1, y, i):
        r = slice(i * sub, (i + 1) * sub)
        z = DEEPNORM_ALPHA * x1[r] + gate * y[r]
        out_ref[0, c * rb + i * sub:c * rb + (i + 1) * sub, :] = _layer_norm_rows(
            z, fg_ref[...], fb_ref[...])

    def up(hb, f):
        a = jnp.dot(hb, wup_ref[:, f * tf:(f + 1) * tf], preferred_element_type=F32)
        g = jnp.dot(hb, wup_ref[:, D_FF + f * tf:D_FF + (f + 1) * tf], preferred_element_type=F32)
        return a, g

    def down(c, f, a, g, prev):
        cols = slice(f * tf, (f + 1) * tf)
        p1 = prev[CONV_HALO - 1:CONV_HALO]
        p2 = prev[CONV_HALO - 2:CONV_HALO - 1]
        g1 = jnp.where(rowid == 0, p1, pltpu.roll(g, 1, axis=0))
        g2 = jnp.where(rowid == 0, p2, jnp.where(rowid == 1, p1, pltpu.roll(g, 2, axis=0)))
        cw = cw_ref[:, cols]
        gc = cw[0:1] * g2 + cw[1:2] * g1 + cw[2:3] * g
        act = a * (0.5 * gc * (1.0 + lax.erf(gc * sqrt_half)))
        return jnp.dot(act.astype(BF16), wdn_ref[cols, :], preferred_element_type=F32)

    ymix = [project(c) for c in range(n_blocks)]
    parts = [norm_in(0, ymix[0], i) for i in range(FFN_SIDE_SPLIT)]
    x1 = [jnp.concatenate([p[0] for p in parts], axis=0)] + [None] * (n_blocks - 1)
    hb = [jnp.concatenate([p[1] for p in parts], axis=0)] + [None] * (n_blocks - 1)
    yffn = [None] * n_blocks
    tails = [None] * n_f
    nxt = up(hb[0], 0)
    for c in range(n_blocks):
        side = []
        if c + 1 < n_blocks:
            side += [("in", i) for i in range(FFN_SIDE_SPLIT)]
        if c > 0:
            side += [("out", i) for i in range(FFN_SIDE_SPLIT)]
        in_parts = []
        y = None
        for f in range(n_f):
            a, g = nxt
            if f + 1 < n_f:
                nxt = up(hb[c], f + 1)
            elif c + 1 < n_blocks:
                nxt = up(hb[c + 1], 0)
            cols = slice(f * tf, (f + 1) * tf)
            prev = halo[:, cols] if c == 0 else tails[f]
            tails[f] = g[rb - CONV_HALO:rb, :]
            if c == n_blocks - 1:
                halo[:, cols] = tails[f]
            part = down(c, f, a, g, prev)
            y = part if y is None else y + part
            if f < len(side):
                kind, i = side[f]
                if kind == "in":
                    in_parts.append(norm_in(c + 1, ymix[c + 1], i))
                    if i == FFN_SIDE_SPLIT - 1:
                        x1[c + 1] = jnp.concatenate([p[0] for p in in_parts], axis=0)
                        hb[c + 1] = jnp.concatenate([p[1] for p in in_parts], axis=0)
                else:
                    norm_out(c - 1, x1[c - 1], yffn[c - 1], i)
        assert len(side) <= n_f
        yffn[c] = y
    for i in range(FFN_SIDE_SPLIT):
        norm_out(n_blocks - 1, x1[n_blocks - 1], yffn[n_blocks - 1], i)


def _mix_ffn(x, ys, mix_mod, mix_g, mix_b, w_out, ffn_mod, ffn_g, ffn_b, w_up, conv_w, w_down):
    bsz, seq, d = x.shape
    ts, tf = FFN_SEQ_TILE, FFN_TILE
    assert D_FF % tf == 0 and seq % ts == 0
    widths = tuple(y.shape[-1] for y in ys)
    tok = lambda w: pl.BlockSpec((1, ts, w), lambda b, s: (b, s, 0))
    mod_spec = pl.BlockSpec((1, 3, d), lambda b, s: (b, 0, 0))
    return pl.pallas_call(
        functools.partial(_mix_ffn_kernel, widths=widths, ts=ts, tf=tf),
        grid=(bsz, seq // ts),
        in_specs=[
            tok(d),
            mod_spec, _const_spec((1, d)), _const_spec((1, d)), _const_spec(w_out.shape),
            mod_spec, _const_spec((1, d)), _const_spec((1, d)),
            _const_spec(w_up.shape), _const_spec(conv_w.shape), _const_spec(w_down.shape),
        ] + [tok(w) for w in widths],
        out_specs=tok(d),
        out_shape=jax.ShapeDtypeStruct((bsz, seq, d), F32),
        scratch_shapes=[pltpu.VMEM((CONV_HALO, D_FF), F32)],
        compiler_params=_params(("parallel", "arbitrary"), 56),
        name="mix_ffn",
    )(x, mix_mod, mix_g, mix_b, w_out, ffn_mod, ffn_g, ffn_b, w_up, conv_w, w_down, *ys)


def _swap_halves(w):
    half = w.shape[-1] // 2
    return jnp.concatenate([w[..., half:], w[..., :half]], axis=-1)


def kernel(x, c, positions, e_ada_w, e_ada_b, e_w_in, e_pool_w, e_pool_scale, e_conv_qk, e_gate_b, e_head_norm, e_w_out, e_ln_g, e_ln_b, o_ada_w, o_ada_b, o_w_in, o_q_norm, o_kv_norm, o_w_uq, o_w_ukv, o_w_out, o_ln_g, o_ln_b, f_ada_w, f_ada_b, f_w_up, f_conv, f_w_down, f_ln_g, f_ln_b):
    bsz, seq, d = x.shape
    row = lambda a: a.reshape(1, -1)

    half = MLA_ROPE // 2
    inv = ROPE_THETA ** (-jnp.arange(half, dtype=F32) / half)
    inv_col = inv.reshape(half, 1)
    pos = positions.astype(F32).reshape(bsz, 1, seq)

    for layer in range(DEPTH):
        j = layer // 2
        if layer % 2 == 0:
            mod = _ada_modulation(c, e_ada_w, e_ada_b, j)
            w_in = e_w_in[j]
            v0 = POOL_WIDTH + 2 * MLSTM_WIDTH
            gate_pad = jnp.zeros((d, GATE_PAD - 2 * MLSTM_HEADS), F32)
            w_cat = jnp.concatenate(
                [w_in[:, :v0], w_in[:, v0 + MLSTM_WIDTH:], gate_pad], axis=1).astype(BF16)
            w_vt = w_in[:, v0:v0 + MLSTM_WIDTH].T.astype(BF16)
            gate_b = jnp.pad(e_gate_b[j], (0, GATE_PAD - 2 * MLSTM_HEADS)).reshape(1, GATE_PAD)
            y_pool, y_mlstm = _even_mixer(
                x, mod, w_cat, w_vt, gate_b, e_pool_w[j].astype(BF16), row(e_pool_scale[j]),
                e_conv_qk[j], row(e_head_norm[j]))
            ys = (y_pool, y_mlstm)
            mix = (mod, row(e_ln_g[j]), row(e_ln_b[j]), e_w_out[j].astype(BF16))
        else:
            mod = _ada_modulation(c, o_ada_w, o_ada_b, j)
            w_in = o_w_in[j]
            k_r = w_in[:, MLA_Q_LORA + MLA_KV_LORA:]
            w_in_cat = jnp.concatenate([w_in, _swap_halves(k_r)], axis=1).astype(BF16)
            wq = o_w_uq[j].reshape(MLA_Q_LORA, MLA_HEADS, MLA_QK)
            wq_rope = wq[:, :, MLA_NOPE:]
            wq_cat = jnp.concatenate([
                wq[:, :, :MLA_NOPE].reshape(MLA_Q_LORA, -1),
                wq_rope.reshape(MLA_Q_LORA, -1),
                _swap_halves(wq_rope).reshape(MLA_Q_LORA, -1)], axis=1).astype(BF16)
            wkv = o_w_ukv[j].reshape(MLA_KV_LORA, MLA_HEADS, MLA_NOPE + MLA_V)
            wk = wkv[:, :, :MLA_NOPE].reshape(MLA_KV_LORA, -1).astype(BF16)
            wvt = wkv[:, :, MLA_NOPE:].reshape(MLA_KV_LORA, -1).T.astype(BF16)
            qh, kh, vth = _mla_proj(x, mod, pos, inv_col, w_in_cat, row(o_q_norm[j]),
                                    row(o_kv_norm[j]), wq_cat, wk, wvt)
            ys = (_attention(qh, kh, vth),)
            mix = (mod, row(o_ln_g[j]), row(o_ln_b[j]), o_w_out[j].astype(BF16))
        ffn_mod = _ada_modulation(c, f_ada_w, f_ada_b, layer)
        x = _mix_ffn(x, ys, *mix, ffn_mod, row(f_ln_g[layer]), row(f_ln_b[layer]),
                     f_w_up[layer].astype(BF16), f_conv[layer], f_w_down[layer].astype(BF16))
    return x
```

```python
import functools

import jax
import jax.numpy as jnp
from jax import lax
from jax.experimental import pallas as pl
from jax.experimental.pallas import tpu as pltpu

F32 = jnp.float32
BF16 = jnp.bfloat16
HIGHEST = lax.Precision.HIGHEST

D_MODEL = 1024
DEPTH = 2
POOL_WINDOWS = (2, 4, 8, 16)
POOL_GROUP = 128
POOL_WIDTH = 512
POOL_HALO = 16
MLSTM_HEADS = 4
MLSTM_HEAD_DIM = 128
MLSTM_WIDTH = 512
MLSTM_CONV = 4
CONV_HALO = 8
GATE_PAD = 128
MLA_HEADS = 8
MLA_NOPE = 128
MLA_ROPE = 64
MLA_V = 128
MLA_QK = MLA_NOPE + MLA_ROPE
MLA_Q_LORA = 512
MLA_KV_LORA = 256
ROPE_THETA = 10000.0
D_FF = 2816
FFN_CONV = 3
DEEPNORM_ALPHA = (2 * DEPTH) ** 0.25
LN_EPS = 1e-5
RMS_EPS = 1e-6
LOG2_E = 1.4426950408889634

VMEM_CAP_BYTES = 64 * 1024 * 1024
LANES = 128
SUBLANES = 8
REDUCE_CHAINS = 4

SEQ_TILE = 512
EVEN_SEQ_TILE = 1024
MLSTM_CHUNK = 256
ATT_TILE = 512
MLA_ROW_BLOCKS = 2
ATT_Q_BLOCKS = 8
ATT_LOOKAHEAD = 3
ATT_CHUNK = 64
ATT_STALE_MARGIN = 64.0
FFN_TILE = 256
FFN_SEQ_TILE = 1024
FFN_ROW_BLOCKS = 4
FFN_SIDE_SPLIT = 4
ADA_TILE = 512


def _params(semantics, vmem_mib):
    limit = vmem_mib * 1024 * 1024
    assert limit <= VMEM_CAP_BYTES
    return pltpu.CompilerParams(dimension_semantics=semantics, vmem_limit_bytes=limit)


def _const_spec(shape):
    nd = len(shape)
    return pl.BlockSpec(shape, lambda *_: (0,) * nd, pipeline_mode=pl.Buffered(1))


def _layer_norm_rows(z, g, b):
    mu = jnp.mean(z, axis=-1, keepdims=True)
    zc = z - mu
    var = jnp.mean(zc * zc, axis=-1, keepdims=True)
    return zc * lax.rsqrt(var + LN_EPS) * g + b


def _sigmoid(x):
    return 1.0 / (1.0 + jnp.exp(-x))


def _tree_reduce(op, a):
    slab = SUBLANES * REDUCE_CHAINS
    if a.shape[0] > slab and a.shape[0] % slab == 0:
        acc = a[:slab]
        for r0 in range(slab, a.shape[0], slab):
            acc = op(acc, a[r0:r0 + slab])
        a = acc
    while a.shape[0] > SUBLANES:
        half = a.shape[0] // 2
        a = op(a[:half], a[half:])
    return a


def _split_bf16(a):
    hi = a.astype(BF16)
    r1 = a - hi.astype(F32)
    mid = r1.astype(BF16)
    lo = (r1 - mid.astype(F32)).astype(BF16)
    return hi, mid, lo


def _ada_kernel(c_ref, w_ref, b_ref, o_ref):
    c = c_ref[...]
    cs = c * _sigmoid(c)
    o_ref[...] = jnp.dot(cs, w_ref[...], precision=HIGHEST, preferred_element_type=F32) + b_ref[...]


def _ada_modulation(c, w, b, layer):
    bsz, d = c.shape
    n_out = w.shape[-1]
    b3 = b.reshape(b.shape[0], 1, n_out)
    out = pl.pallas_call(
        _ada_kernel,
        grid=(n_out // ADA_TILE,),
        in_specs=[
            pl.BlockSpec((bsz, d), lambda n: (0, 0)),
            pl.BlockSpec((None, d, ADA_TILE), lambda n: (layer, 0, n)),
            pl.BlockSpec((None, 1, ADA_TILE), lambda n: (layer, 0, n)),
        ],
        out_specs=pl.BlockSpec((bsz, ADA_TILE), lambda n: (0, n)),
        out_shape=jax.ShapeDtypeStruct((bsz, n_out), F32),
        compiler_params=_params(("arbitrary",), 16),
        name="ada_modulation",
    )(c, w, b3)
    return out.reshape(bsz, 3, d)


def _even_mixer_kernel(x_ref, mod_ref, w_ref, wvt_ref, gb_ref, pw_ref, ps_ref, cw_ref, hn_ref,
                       ypool_ref, ymlstm_ref, pbuf, qkbuf, st_sc, m_sc, *, ts, chunk):
    s = pl.program_id(1)

    @pl.when(s == 0)
    def _():
        pbuf[0:POOL_HALO, :] = jnp.zeros((POOL_HALO, POOL_WIDTH), F32)
        qkbuf[0:CONV_HALO, :] = jnp.zeros((CONV_HALO, 2 * MLSTM_WIDTH), F32)
        st_sc[...] = jnp.zeros(st_sc.shape, F32)
        m_sc[...] = jnp.zeros(m_sc.shape, F32)

    mod = mod_ref[0]
    shift, scale = mod[0:1], mod[1:2]
    h = (x_ref[0] * (1.0 + scale) + shift).astype(BF16)

    nt = (((1,), (1,)), ((), ()))
    c_qk = POOL_WIDTH
    c_o = c_qk + 2 * MLSTM_WIDTH
    c_g = c_o + MLSTM_WIDTH
    dh = MLSTM_HEAD_DIM
    key = lax.broadcasted_iota(jnp.int32, (chunk, chunk), 0)
    qry = lax.broadcasted_iota(jnp.int32, (chunk, chunk), 1)
    visible = key <= qry
    lower = (qry <= key).astype(BF16)
    upper = visible.astype(BF16)
    cw = cw_ref[...]

    def project_steps(c, out):
        r0 = c * chunk
        hc = h[r0:r0 + chunk]

        def qk_step():
            qkbuf[CONV_HALO + r0:CONV_HALO + r0 + chunk, :] = jnp.dot(
                hc, w_ref[:, c_qk:c_o], preferred_element_type=F32)

        def gate_step():
            out["gates"] = (jnp.dot(hc, w_ref[:, c_g:c_g + GATE_PAD], preferred_element_type=F32)
                            + gb_ref[...])

        def v_step():
            out["vt"] = lax.dot_general(wvt_ref[...], hc, nt,
                                        preferred_element_type=F32).astype(BF16)

        def o_step():
            out["o_sig"] = _sigmoid(jnp.dot(hc, w_ref[:, c_o:c_g], preferred_element_type=F32))

        def pool_step():
            pbuf[POOL_HALO + r0:POOL_HALO + r0 + chunk, :] = jnp.dot(
                hc, w_ref[:, 0:c_qk], preferred_element_type=F32)

        return [qk_step, gate_step, v_step, o_step, pool_step]

    def mix_steps(c, inp):
        r0 = c * chunk
        rows = slice(r0, r0 + chunk)
        loc = {}

        def prep_step():
            gates = inp["gates"]
            lane = lax.broadcasted_iota(jnp.int32, gates.shape, 1)
            log_f = jnp.minimum(gates, 0.0) - jnp.log1p(jnp.exp(-jnp.abs(gates)))
            gc = jnp.where(lane >= MLSTM_HEADS, log_f, gates)
            gr = gc.T[0:2 * MLSTM_HEADS, :]
            acc = qkbuf[CONV_HALO + r0:CONV_HALO + r0 + chunk, :] * cw[MLSTM_CONV - 1:MLSTM_CONV]
            for j in range(MLSTM_CONV - 1):
                off = CONV_HALO - (MLSTM_CONV - 1) + j + r0
                acc = acc + qkbuf[off:off + chunk, :] * cw[j:j + 1]
            qk = acc * _sigmoid(acc)
            loc["q"] = qk[:, :MLSTM_WIDTH].astype(BF16)
            loc["k"] = (qk[:, MLSTM_WIDTH:] * (dh ** -0.5)).astype(BF16)
            loc["bc_col"] = sum(jnp.dot(lower, part, preferred_element_type=F32)
                                for part in _split_bf16(gc))
            loc["bc_row"] = sum(jnp.dot(part, upper, preferred_element_type=F32)
                                for part in _split_bf16(gr))
            loc["gc"], loc["gr"] = gc, gr

        def head_step(hd):
            gc, gr = loc["gc"], loc["gr"]
            cols = slice(hd * dh, (hd + 1) * dh)
            c_col = gc[:, hd:hd + 1] - loc["bc_col"][:, MLSTM_HEADS + hd:MLSTM_HEADS + hd + 1]
            i_row = gr[hd:hd + 1, :]
            b_row = loc["bc_row"][MLSTM_HEADS + hd:MLSTM_HEADS + hd + 1, :]
            m_prev = m_sc[hd][0:1, 0:1]
            state = st_sc[hd]
            qh = loc["q"][:, cols]
            kh = loc["k"][:, cols]
            vth = inp["vt"][cols, :]

            dmat = jnp.where(visible, b_row + c_col, -jnp.inf)
            m_inter = b_row + m_prev
            m_t = jnp.maximum(m_inter,
                              jnp.max(_tree_reduce(jnp.maximum, dmat), axis=0, keepdims=True))
            decay = jnp.exp(dmat - m_t)
            inter = jnp.exp(m_inter - m_t)
            scores = lax.dot_general(kh, qh, nt, preferred_element_type=F32) * decay
            sq = lax.dot_general(state.astype(BF16), qh, nt, preferred_element_type=F32)
            num = jnp.dot(vth, scores.astype(BF16), preferred_element_type=F32) + inter * sq[0:dh]
            den = (jnp.sum(_tree_reduce(jnp.add, scores), axis=0, keepdims=True)
                   + inter * sq[dh:dh + 1])
            hc = num / jnp.maximum(jnp.abs(den), jnp.exp(-m_t))

            mu = jnp.sum(_tree_reduce(jnp.add, hc), axis=0, keepdims=True) * (1.0 / dh)
            hz = hc - mu
            var = jnp.sum(_tree_reduce(jnp.add, hz * hz), axis=0, keepdims=True) * (1.0 / dh)
            hn = (hz * lax.rsqrt(var + LN_EPS)).T * hn_ref[:, cols]
            ymlstm_ref[0, rows, cols] = (inp["o_sig"][:, cols] * hn).astype(BF16)

            b_last = b_row[:, chunk - 1:chunk]
            g = b_last - b_row + i_row
            m_new = jnp.maximum(b_last + m_prev, jnp.max(g, axis=1, keepdims=True))
            wk = jnp.exp(g - m_new)
            carry = jnp.exp(b_last + m_prev - m_new)
            lhs = jnp.concatenate(
                [vth.astype(F32) * wk, jnp.broadcast_to(wk, (SUBLANES, chunk))], axis=0)
            st_sc[hd] = carry * state + jnp.dot(lhs.astype(BF16), kh, preferred_element_type=F32)
            m_sc[hd] = jnp.broadcast_to(m_new, (SUBLANES, LANES))

        def pool_step():
            t_glob = s * ts + r0 + lax.broadcasted_iota(jnp.int32, (chunk, 1), 0)
            for gi, win in enumerate(POOL_WINDOWS):
                cols = slice(gi * POOL_GROUP, (gi + 1) * POOL_GROUP)
                acc = pbuf[r0:r0 + POOL_HALO + chunk, cols]
                cur = acc[POOL_HALO:]
                shift = 1
                while shift < win:
                    acc = acc + pltpu.roll(acc, shift, axis=0)
                    shift *= 2
                acc = acc[POOL_HALO:]
                cnt = jnp.minimum(t_glob + 1, win).astype(F32)
                pooled = acc / cnt - cur
                mixed = jnp.dot(pooled.astype(BF16), pw_ref[gi], preferred_element_type=F32)
                ypool_ref[0, rows, cols] = (mixed * ps_ref[:, cols]).astype(BF16)

        heads = [functools.partial(head_step, hd) for hd in range(MLSTM_HEADS)]
        return [prep_step] + heads + [pool_step]

    n_blocks = ts // chunk
    values = [dict() for _ in range(n_blocks)]
    for step in project_steps(0, values[0]):
        step()
    for c in range(n_blocks):
        ahead = project_steps(c + 1, values[c + 1]) if c + 1 < n_blocks else []
        steps = mix_steps(c, values[c])
        assert len(ahead) <= len(steps)
        for i, step in enumerate(steps):
            step()
            if i < len(ahead):
                ahead[i]()
    qkbuf[0:CONV_HALO, :] = qkbuf[ts:ts + CONV_HALO, :]
    pbuf[0:POOL_HALO, :] = pbuf[ts:ts + POOL_HALO, :]


def _even_mixer(x, mod, w_cat, w_vt, gate_b, pool_w, pool_scale, conv_qk, head_norm):
    bsz, seq, d = x.shape
    ts = EVEN_SEQ_TILE
    assert seq % ts == 0 and ts % MLSTM_CHUNK == 0
    tok = lambda width: pl.BlockSpec((1, ts, width), lambda b, s: (b, s, 0))
    return pl.pallas_call(
        functools.partial(_even_mixer_kernel, ts=ts, chunk=MLSTM_CHUNK),
        grid=(bsz, seq // ts),
        in_specs=[
            tok(d),
            pl.BlockSpec((1, 3, d), lambda b, s: (b, 0, 0)),
            _const_spec(w_cat.shape),
            _const_spec(w_vt.shape),
            _const_spec((1, GATE_PAD)),
            _const_spec(pool_w.shape),
            _const_spec((1, POOL_WIDTH)),
            _const_spec(conv_qk.shape),
            _const_spec((1, MLSTM_WIDTH)),
        ],
        out_specs=(tok(POOL_WIDTH), tok(MLSTM_WIDTH)),
        out_shape=(
            jax.ShapeDtypeStruct((bsz, seq, POOL_WIDTH), BF16),
            jax.ShapeDtypeStruct((bsz, seq, MLSTM_WIDTH), BF16),
        ),
        scratch_shapes=[
            pltpu.VMEM((POOL_HALO + ts, POOL_WIDTH), F32),
            pltpu.VMEM((CONV_HALO + ts, 2 * MLSTM_WIDTH), F32),
            pltpu.VMEM((MLSTM_HEADS, MLSTM_HEAD_DIM + SUBLANES, MLSTM_HEAD_DIM), F32),
            pltpu.VMEM((MLSTM_HEADS, SUBLANES, LANES), F32),
        ],
        compiler_params=_params(("parallel", "arbitrary"), 48),
        name="even_mixer",
    )(x, mod, w_cat, w_vt, gate_b, pool_w, pool_scale, conv_qk, head_norm)


def _mla_proj_kernel(x_ref, mod_ref, pos_ref, inv_ref, w_in_ref, qn_ref, kvn_ref, wq_ref, wk_ref, wvt_ref,
                     q_out, k_out, vt_out, *, ts):
    mod = mod_ref[0]
    shift, scale = mod[0:1], mod[1:2]
    h = (x_ref[0] * (1.0 + scale) + shift).astype(BF16)

    ang = inv_ref[...] * pos_ref[0]
    cos_t = jnp.cos(ang)
    sin_t = jnp.sin(ang)
    pairs = LANES // MLA_ROPE
    cos_all = jnp.concatenate([cos_t, cos_t] * pairs, axis=0).T
    sin_all = jnp.concatenate([-sin_t, sin_t] * pairs, axis=0).T

    def rms(a, g):
        return a * lax.rsqrt(jnp.mean(a * a, axis=-1, keepdims=True) + RMS_EPS) * g

    sm_scale = MLA_QK ** -0.5 * LOG2_E
    n_nope = MLA_HEADS * MLA_NOPE
    n_rope = MLA_HEADS * MLA_ROPE
    reps = n_rope // LANES
    nt = (((1,), (1,)), ((), ()))

    rb = ts // MLA_ROW_BLOCKS
    blocks = [slice(i * rb, (i + 1) * rb) for i in range(MLA_ROW_BLOCKS)]
    us = [jnp.dot(h[r], w_in_ref[...], preferred_element_type=F32) for r in blocks]
    for r, u in zip(blocks, us):
        cq = rms(u[:, :MLA_Q_LORA], qn_ref[...]).astype(BF16)
        ckv = rms(u[:, MLA_Q_LORA:MLA_Q_LORA + MLA_KV_LORA], kvn_ref[...]).astype(BF16)
        kr = u[:, MLA_Q_LORA + MLA_KV_LORA:]
        cos, sin = cos_all[r], sin_all[r]
        qa = jnp.dot(cq, wq_ref[...], preferred_element_type=F32) * sm_scale
        q_rope = (qa[:, n_nope:n_nope + n_rope] * jnp.concatenate([cos] * reps, axis=1)
                  + qa[:, n_nope + n_rope:] * jnp.concatenate([sin] * reps, axis=1))
        k_rope = (kr[:, :MLA_ROPE] * cos[:, :MLA_ROPE]
                  + kr[:, MLA_ROPE:] * sin[:, :MLA_ROPE]).astype(BF16)
        k_nope = jnp.dot(ckv, wk_ref[...], preferred_element_type=F32)
        vt = lax.dot_general(wvt_ref[...], ckv, nt, preferred_element_type=F32)
        for hd in range(MLA_HEADS):
            q_out[0, hd, r, 0:MLA_NOPE] = qa[:, hd * MLA_NOPE:(hd + 1) * MLA_NOPE].astype(BF16)
            q_out[0, hd, r, MLA_NOPE:MLA_QK] = q_rope[:, hd * MLA_ROPE:(hd + 1) * MLA_ROPE].astype(BF16)
            k_out[0, hd, r, 0:MLA_NOPE] = k_nope[:, hd * MLA_NOPE:(hd + 1) * MLA_NOPE].astype(BF16)
            k_out[0, hd, r, MLA_NOPE:MLA_QK] = k_rope
            vt_out[0, hd, 0, :, r] = vt[hd * MLA_V:(hd + 1) * MLA_V, :].astype(BF16)


def _mla_proj(x, mod, pos, inv, w_in, q_norm, kv_norm, w_uq, w_uk, w_uvt):
    bsz, seq, d = x.shape
    ts = SEQ_TILE
    assert ts == ATT_TILE
    head_spec = lambda w: pl.BlockSpec((1, MLA_HEADS, ts, w), lambda b, s: (b, 0, s, 0))
    return pl.pallas_call(
        functools.partial(_mla_proj_kernel, ts=ts),
        grid=(bsz, seq // ts),
        in_specs=[
            pl.BlockSpec((1, ts, d), lambda b, s: (b, s, 0)),
            pl.BlockSpec((1, 3, d), lambda b, s: (b, 0, 0)),
            pl.BlockSpec((1, 1, ts), lambda b, s: (b, 0, s)),
            _const_spec(inv.shape),
            _const_spec(w_in.shape),
            _const_spec((1, MLA_Q_LORA)), _const_spec((1, MLA_KV_LORA)),
            _const_spec(w_uq.shape), _const_spec(w_uk.shape), _const_spec(w_uvt.shape),
        ],
        out_specs=(
            head_spec(MLA_QK), head_spec(MLA_QK),
            pl.BlockSpec((1, MLA_HEADS, 1, MLA_V, ts), lambda b, s: (b, 0, s, 0, 0)),
        ),
        out_shape=(
            jax.ShapeDtypeStruct((bsz, MLA_HEADS, seq, MLA_QK), BF16),
            jax.ShapeDtypeStruct((bsz, MLA_HEADS, seq, MLA_QK), BF16),
            jax.ShapeDtypeStruct((bsz, MLA_HEADS, seq // ts, MLA_V, ts), BF16),
        ),
        compiler_params=_params(("parallel", "arbitrary"), 48),
        name="mla_proj",
    )(x, mod, pos, inv, w_in, q_norm, kv_norm, w_uq, w_uk, w_uvt)


def _attn_kernel(q_ref, k_ref, vt_ref, o_ref, m_sc, l_sc, acc_sc, *, tile, q_blocks):
    qi = pl.program_id(2)
    nt = (((1,), (1,)), ((), ()))

    def score_tile(chain):
        j, qb, _ = chain
        k = k_ref[0, 0, pl.ds(pl.multiple_of(j * tile, tile), tile), :]
        q = q_ref[0, 0, qb * tile:(qb + 1) * tile, :]
        return lax.dot_general(k, q, nt, preferred_element_type=F32)

    def for_each_chain(chains, consume):
        scores = []
        for idx, chain in enumerate(chains):
            while len(scores) < min(len(chains), idx + 1 + ATT_LOOKAHEAD):
                scores.append(score_tile(chains[len(scores)]))
            consume(chain, scores[idx])

    def col_max(a):
        return jnp.max(_tree_reduce(jnp.maximum, a), axis=0, keepdims=True)

    def col_sum(a):
        return jnp.sum(_tree_reduce(jnp.add, a), axis=0, keepdims=True)

    def pv(j, p):
        return jnp.dot(vt_ref[0, 0, j], p.astype(BF16), preferred_element_type=F32)

    def load_state(side):
        return [(m_sc[side, qb], l_sc[side, qb], acc_sc[side, qb]) for qb in range(q_blocks)]

    def store_state(side, state):
        for qb, (m, l, acc) in enumerate(state):
            m_sc[side, qb] = m
            l_sc[side, qb] = l
            acc_sc[side, qb] = acc

    def exact_update(state, chains):
        state = list(state)

        def consume(chain, sc):
            j, qb, masked = chain
            if masked:
                kc = lax.broadcasted_iota(jnp.int32, sc.shape, 0) // ATT_CHUNK
                qc = lax.broadcasted_iota(jnp.int32, sc.shape, 1) // ATT_CHUNK
                sc = jnp.where(qc >= kc, sc, -jnp.inf)
            m_prev, l_prev, acc_prev = state[qb]
            m_new = jnp.maximum(m_prev, col_max(sc))
            alpha = jnp.exp2(m_prev - m_new)
            p = jnp.exp2(sc - m_new)
            state[qb] = (m_new, alpha * l_prev + col_sum(p), alpha * acc_prev + pv(j, p))

        for_each_chain(chains, consume)
        return state

    def stale_update(state, chains):
        m0 = [st[0] for st in state]
        l = [st[1] for st in state]
        acc = [st[2] for st in state]
        top = [None] * q_blocks

        def consume(chain, sc):
            j, qb, _ = chain
            p = jnp.exp2(sc - m0[qb])
            tile_max = col_max(sc)
            l[qb] = l[qb] + col_sum(p)
            acc[qb] = acc[qb] + pv(j, p)
            top[qb] = tile_max if top[qb] is None else jnp.maximum(top[qb], tile_max)

        for_each_chain(chains, consume)
        new_state = []
        worst = None
        for qb in range(q_blocks):
            if top[qb] is None:
                new_state.append(state[qb])
                continue
            m_new = jnp.maximum(m0[qb], top[qb])
            alpha = jnp.exp2(m0[qb] - m_new)
            new_state.append((m_new, alpha * l[qb], alpha * acc[qb]))
            excess = top[qb] - m0[qb]
            worst = excess if worst is None else jnp.maximum(worst, excess)
        return new_state, worst

    j0 = qi * q_blocks
    empty = [(jnp.full((1, tile), -jnp.inf, F32), jnp.zeros((1, tile), F32),
              jnp.zeros((acc_sc.shape[2], tile), F32)) for _ in range(q_blocks)]
    store_state(0, exact_update(empty, [(j0 + qb, qb, True) for qb in range(q_blocks)]))

    def stale_group(src, chains):
        state, worst = stale_update(load_state(src), chains)
        store_state(1 - src, state)

        @pl.when(jnp.max(worst) > ATT_STALE_MARGIN)
        def _():
            store_state(1 - src, exact_update(load_state(src), chains))

    below = [(j0 + dj, qb, False) for dj in range(q_blocks) for qb in range(dj + 1, q_blocks)]
    if below:
        stale_group(0, below)
    first = 1 if below else 0

    def body(i, carry):
        stale_group((first + i) % 2,
                    [(i * q_blocks + dj, qb, False) for dj in range(q_blocks) for qb in range(q_blocks)])
        return carry

    lax.fori_loop(0, qi, body, 0)
    last = (first + qi) % 2
    for qb in range(q_blocks):
        o_ref[0, qb * tile:(qb + 1) * tile, :] = (acc_sc[last, qb] / l_sc[last, qb]).T.astype(BF16)


def _attention(q, k, vt):
    bsz, heads, seq, dqk = q.shape
    n_kv, dv, tile = vt.shape[2:]
    q_blocks = ATT_Q_BLOCKS
    tq = tile * q_blocks
    return pl.pallas_call(
        functools.partial(_attn_kernel, tile=tile, q_blocks=q_blocks),
        grid=(bsz, heads, seq // tq),
        in_specs=[
            pl.BlockSpec((1, 1, tq, dqk), lambda b, h, i: (b, h, i, 0)),
            pl.BlockSpec((1, 1, seq, dqk), lambda b, h, i: (b, h, 0, 0)),
            pl.BlockSpec((1, 1, n_kv, dv, tile), lambda b, h, i: (b, h, 0, 0, 0)),
        ],
        out_specs=pl.BlockSpec((1, tq, dv), lambda b, h, i: (b, i, h)),
        out_shape=jax.ShapeDtypeStruct((bsz, seq, heads * dv), BF16),
        scratch_shapes=[
            pltpu.VMEM((2, q_blocks, 1, tile), F32),
            pltpu.VMEM((2, q_blocks, 1, tile), F32),
            pltpu.VMEM((2, q_blocks, dv, tile), F32),
        ],
        compiler_params=_params(("parallel", "parallel", "arbitrary"), 48),
        name="mla_attention",
    )(q, k, vt)


def _mix_ffn_kernel(*refs, widths, ts, tf):
    (x_ref, mmod_ref, mg_ref, mb_ref, wo_ref,
     fmod_ref, fg_ref, fb_ref, wup_ref, cw_ref, wdn_ref) = refs[:11]
    y_refs = refs[11:11 + len(widths)]
    out_ref, halo = refs[11 + len(widths):]
    s = pl.program_id(1)

    @pl.when(s == 0)
    def _():
        halo[...] = jnp.zeros(halo.shape, F32)

    fmod = fmod_ref[0]
    shift, scale, gate = fmod[0:1], fmod[1:2], fmod[2:3]
    mgate = mmod_ref[0][2:3]
    n_blocks = FFN_ROW_BLOCKS
    rb = ts // n_blocks
    sub = rb // FFN_SIDE_SPLIT
    n_f = D_FF // tf
    rowid = lax.broadcasted_iota(jnp.int32, (rb, tf), 0)
    sqrt_half = 0.5 ** 0.5

    def project(c):
        y = None
        r0 = 0
        for y_ref, width in zip(y_refs, widths):
            part = jnp.dot(y_ref[0, c * rb:(c + 1) * rb, :], wo_ref[r0:r0 + width, :],
                           preferred_element_type=F32)
            y = part if y is None else y + part
            r0 += width
        return y

    def norm_in(c, y, i):
        r = slice(c * rb + i * sub, c * rb + (i + 1) * sub)
        x1 = _layer_norm_rows(DEEPNORM_ALPHA * x_ref[0, r, :] + mgate * y[i * sub:(i + 1) * sub],
                              mg_ref[...], mb_ref[...])
        return x1, (x1 * (1.0 + scale) + shift).astype(BF16)

    def norm_out(c, x1, y, i):
        r = slice(i * sub, (i + 1) * sub)
        z = DEEPNORM_ALPHA * x1[r] + gate * y[r]
        out_ref[0, c * rb + i * sub:c * rb + (i + 1) * sub, :] = _layer_norm_rows(
            z, fg_ref[...], fb_ref[...])

    def up(hb, f):
        a = jnp.dot(hb, wup_ref[:, f * tf:(f + 1) * tf], preferred_element_type=F32)
        g = jnp.dot(hb, wup_ref[:, D_FF + f * tf:D_FF + (f + 1) * tf], preferred_element_type=F32)
        return a, g

    def down(c, f, a, g, prev):
        cols = slice(f * tf, (f + 1) * tf)
        p1 = prev[CONV_HALO - 1:CONV_HALO]
        p2 = prev[CONV_HALO - 2:CONV_HALO - 1]
        g1 = jnp.where(rowid == 0, p1, pltpu.roll(g, 1, axis=0))
        g2 = jnp.where(rowid == 0, p2, jnp.where(rowid == 1, p1, pltpu.roll(g, 2, axis=0)))
        cw = cw_ref[:, cols]
        gc = cw[0:1] * g2 + cw[1:2] * g1 + cw[2:3] * g
        act = a * (0.5 * gc * (1.0 + lax.erf(gc * sqrt_half)))
        return jnp.dot(act.astype(BF16), wdn_ref[cols, :], preferred_element_type=F32)

    ymix = [project(c) for c in range(n_blocks)]
    parts = [norm_in(0, ymix[0], i) for i in range(FFN_SIDE_SPLIT)]
    x1 = [jnp.concatenate([p[0] for p in parts], axis=0)] + [None] * (n_blocks - 1)
    hb = [jnp.concatenate([p[1] for p in parts], axis=0)] + [None] * (n_blocks - 1)
    yffn = [None] * n_blocks
    tails = [None] * n_f
    nxt = up(hb[0], 0)
    for c in range(n_blocks):
        side = []
        if c + 1 < n_blocks:
            side += [("in", i) for i in range(FFN_SIDE_SPLIT)]
        if c > 0:
            side += [("out", i) for i in range(FFN_SIDE_SPLIT)]
        in_parts = []
        y = None
        for f in range(n_f):
            a, g = nxt
            if f + 1 < n_f:
                nxt = up(hb[c], f + 1)
            elif c + 1 < n_blocks:
                nxt = up(hb[c + 1], 0)
            cols = slice(f * tf, (f + 1) * tf)
            prev = halo[:, cols] if c == 0 else tails[f]
            tails[f] = g[rb - CONV_HALO:rb, :]
            if c == n_blocks - 1:
                halo[:, cols] = tails[f]
            part = down(c, f, a, g, prev)
            y = part if y is None else y + part
            if f < len(side):
                kind, i = side[f]
                if kind == "in":
                    in_parts.append(norm_in(c + 1, ymix[c + 1], i))
                    if i == FFN_SIDE_SPLIT - 1:
                        x1[c + 1] = jnp.concatenate([p[0] for p in in_parts], axis=0)
                        hb[c + 1] = jnp.concatenate([p[1] for p in in_parts], axis=0)
                else:
                    norm_out(c - 1, x1[c - 1], yffn[c - 1], i)
        assert len(side) <= n_f
        yffn[c] = y
    for i in range(FFN_SIDE_SPLIT):
        norm_out(n_blocks - 1, x1[n_blocks - 1], yffn[n_blocks - 1], i)


def _mix_ffn(x, ys, mix_mod, mix_g, mix_b, w_out, ffn_mod, ffn_g, ffn_b, w_up, conv_w, w_down):
    bsz, seq, d = x.shape
    ts, tf = FFN_SEQ_TILE, FFN_TILE
    assert D_FF % tf == 0 and seq % ts == 0
    widths = tuple(y.shape[-1] for y in ys)
    tok = lambda w: pl.BlockSpec((1, ts, w), lambda b, s: (b, s, 0))
    mod_spec = pl.BlockSpec((1, 3, d), lambda b, s: (b, 0, 0))
    return pl.pallas_call(
        functools.partial(_mix_ffn_kernel, widths=widths, ts=ts, tf=tf),
        grid=(bsz, seq // ts),
        in_specs=[
            tok(d),
            mod_spec, _const_spec((1, d)), _const_spec((1, d)), _const_spec(w_out.shape),
            mod_spec, _const_spec((1, d)), _const_spec((1, d)),
            _const_spec(w_up.shape), _const_spec(conv_w.shape), _const_spec(w_down.shape),
        ] + [tok(w) for w in widths],
        out_specs=tok(d),
        out_shape=jax.ShapeDtypeStruct((bsz, seq, d), F32),
        scratch_shapes=[pltpu.VMEM((CONV_HALO, D_FF), F32)],
        compiler_params=_params(("parallel", "arbitrary"), 56),
        name="mix_ffn",
    )(x, mix_mod, mix_g, mix_b, w_out, ffn_mod, ffn_g, ffn_b, w_up, conv_w, w_down, *ys)


def _swap_halves(w):
    half = w.shape[-1] // 2
    return jnp.concatenate([w[..., half:], w[..., :half]], axis=-1)


def kernel(x, c, positions, e_ada_w, e_ada_b, e_w_in, e_pool_w, e_pool_scale, e_conv_qk, e_gate_b, e_head_norm, e_w_out, e_ln_g, e_ln_b, o_ada_w, o_ada_b, o_w_in, o_q_norm, o_kv_norm, o_w_uq, o_w_ukv, o_w_out, o_ln_g, o_ln_b, f_ada_w, f_ada_b, f_w_up, f_conv, f_w_down, f_ln_g, f_ln_b):
    bsz, seq, d = x.shape
    row = lambda a: a.reshape(1, -1)

    half = MLA_ROPE // 2
    inv = ROPE_THETA ** (-jnp.arange(half, dtype=F32) / half)
    inv_col = inv.reshape(half, 1)
    pos = positions.astype(F32).reshape(bsz, 1, seq)

    for layer in range(DEPTH):
        j = layer // 2
        if layer % 2 == 0:
            mod = _ada_modulation(c, e_ada_w, e_ada_b, j)
            w_in = e_w_in[j]
            v0 = POOL_WIDTH + 2 * MLSTM_WIDTH
            gate_pad = jnp.zeros((d, GATE_PAD - 2 * MLSTM_HEADS), F32)
            w_cat = jnp.concatenate(
                [w_in[:, :v0], w_in[:, v0 + MLSTM_WIDTH:], gate_pad], axis=1).astype(BF16)
            w_vt = w_in[:, v0:v0 + MLSTM_WIDTH].T.astype(BF16)
            gate_b = jnp.pad(e_gate_b[j], (0, GATE_PAD - 2 * MLSTM_HEADS)).reshape(1, GATE_PAD)
            y_pool, y_mlstm = _even_mixer(
                x, mod, w_cat, w_vt, gate_b, e_pool_w[j].astype(BF16), row(e_pool_scale[j]),
                e_conv_qk[j], row(e_head_norm[j]))
            ys = (y_pool, y_mlstm)
            mix = (mod, row(e_ln_g[j]), row(e_ln_b[j]), e_w_out[j].astype(BF16))
        else:
            mod = _ada_modulation(c, o_ada_w, o_ada_b, j)
            w_in = o_w_in[j]
            k_r = w_in[:, MLA_Q_LORA + MLA_KV_LORA:]
            w_in_cat = jnp.concatenate([w_in, _swap_halves(k_r)], axis=1).astype(BF16)
            wq = o_w_uq[j].reshape(MLA_Q_LORA, MLA_HEADS, MLA_QK)
            wq_rope = wq[:, :, MLA_NOPE:]
            wq_cat = jnp.concatenate([
                wq[:, :, :MLA_NOPE].reshape(MLA_Q_LORA, -1),
                wq_rope.reshape(MLA_Q_LORA, -1),
                _swap_halves(wq_rope).reshape(MLA_Q_LORA, -1)], axis=1).astype(BF16)
            wkv = o_w_ukv[j].reshape(MLA_KV_LORA, MLA_HEADS, MLA_NOPE + MLA_V)
            wk = wkv[:, :, :MLA_NOPE].reshape(MLA_KV_LORA, -1).astype(BF16)
            wvt = wkv[:, :, MLA_NOPE:].reshape(MLA_KV_LORA, -1).T.astype(BF16)
            qh, kh, vth = _mla_proj(x, mod, pos, inv_col, w_in_cat, row(o_q_norm[j]),
                                    row(o_kv_norm[j]), wq_cat, wk, wvt)
            ys = (_attention(qh, kh, vth),)
            mix = (mod, row(o_ln_g[j]), row(o_ln_b[j]), o_w_out[j].astype(BF16))
        ffn_mod = _ada_modulation(c, f_ada_w, f_ada_b, layer)
        x = _mix_ffn(x, ys, *mix, ffn_mod, row(f_ln_g[layer]), row(f_ln_b[layer]),
                     f_w_up[layer].astype(BF16), f_conv[layer], f_w_down[layer].astype(BF16))
    return x
```
